```python
import jax, jax.numpy as jnp
from jax import lax
import numpy as np

D_MODEL = 2048
BATCH = 4
SEQ = 4096
DEPTH = 4
DEC_BATCH = 4
DEC_SEQ = 2048
PAST_LEN = 128

GRID_W = 64
N_EVEN = (DEPTH + 1) // 2
N_ODD = DEPTH // 2
D_FF = 5632
N_MOD = 9
NORM_EPS = 1e-6
D_A = D_MODEL // 2
GMLP_GROUPS = 8
GMLP_CHUNK = 128
D_B = D_MODEL - D_A
NA_HEADS = 8
NA_HEAD_DIM = D_B // NA_HEADS
NA_KH = 8
NA_KW = 16
NA_QB = 16
NA_KB = 32
D_IN_AB = 2 * D_A + 3 * D_B
GLA_HEADS = 4
GLA_DK = D_MODEL // 2
GLA_DV = D_MODEL
GLA_HK = GLA_DK // GLA_HEADS
GLA_HV = GLA_DV // GLA_HEADS
GLA_RANK = 16
GLA_TAU = 16.0
GLA_CHUNK = 64
D_IN_C = 2 * GLA_DK + 2 * GLA_DV + 2 * GLA_RANK

kernel_name = "hybrid_bidir_gmlp_natten_gla_encoder"


def rms_norm(x, g):
    x32 = x.astype(jnp.float32)
    y = x32 * lax.rsqrt(jnp.mean(x32 * x32, axis=-1, keepdims=True) + NORM_EPS)
    return y.astype(x.dtype) * g


def layer_norm(x, g, b):
    x32 = x.astype(jnp.float32)
    mu = jnp.mean(x32, axis=-1, keepdims=True)
    xc = x32 - mu
    y = xc * lax.rsqrt(jnp.mean(xc * xc, axis=-1, keepdims=True) + NORM_EPS)
    return y.astype(x.dtype) * g + b


def modulate(h, shift, scale):
    return h * (1.0 + scale) + shift


def swiglu(h, w_gate, w_up, w_down):
    return (jax.nn.silu(h @ w_gate) * (h @ w_up)) @ w_down


def spatial_gating(u, v, w_s, b_s):
    B_, T, _ = v.shape
    nc = T // GMLP_CHUNK
    vc = v.reshape(B_, nc, GMLP_CHUNK, GMLP_GROUPS, D_A // GMLP_GROUPS)
    s = jnp.einsum('gij,bnjgc->bnigc', w_s, vc) + b_s.T[:, :, None]
    return u * s.reshape(B_, T, D_A)


def neighborhood_attention(q, k, v, rpb):
    B_, T, H, Dh = q.shape
    rows = T // GRID_W
    kh = min(NA_KH, rows)
    f32 = jnp.float32
    qg = q.astype(f32).reshape(B_, rows, GRID_W, H, Dh) * (Dh ** -0.5)
    kg = k.astype(f32).reshape(B_, rows, GRID_W, H, Dh)
    vg = v.astype(f32).reshape(B_, rows, GRID_W, H, Dh)
    r = np.arange(rows)
    row_start = np.clip(r - kh // 2, 0, rows - kh)
    row_idx = row_start[:, None] + np.arange(kh)[None, :]
    k_band = jnp.take(kg, row_idx, axis=1)
    v_band = jnp.take(vg, row_idx, axis=1)
    row_bias_idx = row_idx - r[:, None] + (NA_KH - 1)
    outs = []
    for j in range(GRID_W // NA_QB):
        qc = np.arange(j * NA_QB, (j + 1) * NA_QB)
        ks = int(np.clip(j * NA_QB - NA_KW // 2, 0, GRID_W - NA_KB))
        kc = np.arange(ks, ks + NA_KB)
        cstart = np.clip(qc - NA_KW // 2, 0, GRID_W - NA_KW)
        valid = (kc[None, :] >= cstart[:, None]) & (kc[None, :] < cstart[:, None] + NA_KW)
        col_bias_idx = np.clip(kc[None, :] - qc[:, None] + NA_KW - 1, 0, 2 * NA_KW - 2)
        bias = rpb.astype(f32)[:, row_bias_idx[:, None, :, None], col_bias_idx[None, :, None, :]]
        s = jnp.einsum('brqhd,brkwhd->bhrqkw', qg[:, :, j * NA_QB:(j + 1) * NA_QB],
                       k_band[:, :, :, ks:ks + NA_KB]) + bias[None]
        s = jnp.where(valid[:, None, :], s, jnp.float32(-1e30))
        p = jax.nn.softmax(s.reshape(s.shape[:4] + (-1,)), axis=-1).reshape(s.shape)
        outs.append(jnp.einsum('bhrqkw,brkwhd->brqhd', p, v_band[:, :, :, ks:ks + NA_KB]))
    out = jnp.concatenate(outs, axis=2).reshape(B_, T, H, Dh)
    return out.astype(q.dtype)


def ab_mixer(h, w_in, ln_g, ln_b, w_s, b_s, rpb, w_out):
    B_, T, _ = h.shape
    z = h @ w_in
    zu, zv, zq, zk, zvv = jnp.split(z, [D_A, 2 * D_A, 2 * D_A + D_B, 2 * D_A + 2 * D_B], axis=-1)
    u = jax.nn.gelu(zu, approximate=False)
    vs = layer_norm(jax.nn.gelu(zv, approximate=False), ln_g, ln_b)
    a_out = spatial_gating(u, vs, w_s, b_s)
    shp = (B_, T, NA_HEADS, NA_HEAD_DIM)
    b_out = neighborhood_attention(zq.reshape(shp), zk.reshape(shp), zvv.reshape(shp), rpb).reshape(B_, T, D_B)
    return jnp.concatenate([a_out, b_out], axis=-1) @ w_out


def gla_chunked(q, k, v, g, strict):
    B_, T, H, dk = q.shape
    dv = v.shape[-1]
    L = GLA_CHUNK
    nc = T // L
    qc = q.reshape(B_, nc, L, H, dk)
    kc = k.reshape(B_, nc, L, H, dk)
    vc = v.reshape(B_, nc, L, H, dv)
    b = jnp.cumsum(g.reshape(B_, nc, L, H, dk), axis=2)
    b_ref = b[:, :, L // 2:L // 2 + 1]
    A = jnp.einsum('bnihd,bnjhd->bnhij', qc * jnp.exp(b - b_ref), kc * jnp.exp(b_ref - b))
    mask = jnp.tril(jnp.ones((L, L), dtype=bool), k=-1 if strict else 0)
    A = jnp.where(mask, A, 0.0)
    o_intra = jnp.einsum('bnhij,bnjhe->bnihe', A, vc)
    b_last = b[:, :, -1:]
    qx = jnp.moveaxis(qc * jnp.exp(b), 1, 0)
    kx = jnp.moveaxis(kc * jnp.exp(b_last - b), 1, 0)
    vx = jnp.moveaxis(vc, 1, 0)
    dx = jnp.moveaxis(jnp.exp(b_last[:, :, 0]), 1, 0)

    def step(S, xs):
        qi, ki, vi, di = xs
        o = jnp.einsum('blhd,bhde->blhe', qi, S)
        S = di[..., None] * S + jnp.einsum('blhd,blhe->bhde', ki, vi)
        return S, o

    S0 = jnp.zeros((B_, H, dk, dv), dtype=q.dtype)
    _, o_inter = lax.scan(step, S0, (qx, kx, vx, dx))
    o = o_intra + jnp.moveaxis(o_inter, 0, 1)
    return o.reshape(B_, T, H, dv)


def gla_mixer(h, w_in, w_a2, b_a, norm_g, w_out):
    B_, T, _ = h.shape
    f32 = jnp.float32
    z = h @ w_in
    q, k, v, r, a = jnp.split(z, [GLA_DK, 2 * GLA_DK, 2 * GLA_DK + GLA_DV, 2 * GLA_DK + 2 * GLA_DV], axis=-1)
    gate_pre = jnp.einsum('btzr,zrk->btzk', a.reshape(B_, T, 2, GLA_RANK), w_a2) + b_a
    g = jax.nn.log_sigmoid(gate_pre.astype(f32)) / GLA_TAU
    gf = g[:, :, 0].reshape(B_, T, GLA_HEADS, GLA_HK)
    gb = g[:, :, 1].reshape(B_, T, GLA_HEADS, GLA_HK)
    q = q.astype(f32).reshape(B_, T, GLA_HEADS, GLA_HK) * (GLA_HK ** -0.5)
    k = k.astype(f32).reshape(B_, T, GLA_HEADS, GLA_HK)
    v = v.astype(f32).reshape(B_, T, GLA_HEADS, GLA_HV)
    o_f = gla_chunked(q, k, v, gf, strict=False)
    o_b = gla_chunked(q[:, ::-1], k[:, ::-1], v[:, ::-1], gb[:, ::-1], strict=True)[:, ::-1]
    o = o_f + o_b
    o = o * lax.rsqrt(jnp.mean(o * o, axis=-1, keepdims=True) + NORM_EPS)
    o = (o.astype(h.dtype) * norm_g).reshape(B_, T, GLA_DV) * jax.nn.silu(r)
    return o @ w_out


def encoder(x, c, mod_w, mod_b, norm_g, ffn_w_gate, ffn_w_up, ffn_w_down,
            ab_w_in, gmlp_ln_g, gmlp_ln_b, gmlp_w_s, gmlp_b_s, na_rpb, ab_w_out,
            gla_w_in, gla_w_a2, gla_b_a, gla_norm_g, gla_w_out, final_norm_g):
    Bc = c.shape[0]
    for i in range(DEPTH):
        mod = (jax.nn.silu(c) @ mod_w[i] + mod_b[i]).reshape(Bc, N_MOD, 1, D_MODEL)
        h = modulate(rms_norm(x, norm_g[i, 0]), mod[:, 0], mod[:, 1])
        x = x + 0.5 * mod[:, 2] * swiglu(h, ffn_w_gate[i, 0], ffn_w_up[i, 0], ffn_w_down[i, 0])
        h = modulate(rms_norm(x, norm_g[i, 1]), mod[:, 3], mod[:, 4])
        j = i // 2
        if i % 2 == 0:
            y = ab_mixer(h, ab_w_in[j], gmlp_ln_g[j], gmlp_ln_b[j], gmlp_w_s[j], gmlp_b_s[j], na_rpb[j], ab_w_out[j])
        else:
            y = gla_mixer(h, gla_w_in[j], gla_w_a2[j], gla_b_a[j], gla_norm_g[j], gla_w_out[j])
        x = x + mod[:, 5] * y
        h = modulate(rms_norm(x, norm_g[i, 2]), mod[:, 6], mod[:, 7])
        x = x + 0.5 * mod[:, 8] * swiglu(h, ffn_w_gate[i, 1], ffn_w_up[i, 1], ffn_w_down[i, 1])
    return rms_norm(x, final_norm_g)


def setup_inputs(seed: int = 0) -> dict:
    key = jax.random.key(seed)
    ks = jax.random.split(key, 24)
    f32 = jnp.float32

    def nrm(k, shape, scale):
        return jax.random.normal(k, shape, f32) * scale

    D = D_MODEL
    return {
        "x_prompt": nrm(ks[0], (BATCH, SEQ, D), 1.0),
        "x_sample": nrm(ks[1], (DEC_BATCH, DEC_SEQ, D), 1.0),
        "c_prompt": nrm(ks[2], (BATCH, D), 1.0),
        "c_sample": nrm(ks[3], (DEC_BATCH, D), 1.0),
        "mod_w": nrm(ks[4], (DEPTH, D, N_MOD * D), D ** -0.5),
        "mod_b": nrm(ks[5], (DEPTH, N_MOD * D), 0.02),
        "norm_g": 1.0 + nrm(ks[6], (DEPTH, 3, D), 0.02),
        "ffn_w_gate": nrm(ks[7], (DEPTH, 2, D, D_FF), D ** -0.5),
        "ffn_w_up": nrm(ks[8], (DEPTH, 2, D, D_FF), D ** -0.5),
        "ffn_w_down": nrm(ks[9], (DEPTH, 2, D_FF, D), D_FF ** -0.5),
        "ab_w_in": nrm(ks[10], (N_EVEN, D, D_IN_AB), D ** -0.5),
        "gmlp_ln_g": 1.0 + nrm(ks[11], (N_EVEN, D_A), 0.02),
        "gmlp_ln_b": nrm(ks[12], (N_EVEN, D_A), 0.02),
        "gmlp_w_s": nrm(ks[13], (N_EVEN, GMLP_GROUPS, GMLP_CHUNK, GMLP_CHUNK), GMLP_CHUNK ** -0.5),
        "gmlp_b_s": 1.0 + nrm(ks[14], (N_EVEN, GMLP_GROUPS, GMLP_CHUNK), 0.1),
        "na_rpb": nrm(ks[15], (N_EVEN, NA_HEADS, 2 * NA_KH - 1, 2 * NA_KW - 1), 0.1),
        "ab_w_out": nrm(ks[16], (N_EVEN, D_A + D_B, D), (D_A + D_B) ** -0.5),
        "gla_w_in": nrm(ks[17], (N_ODD, D, D_IN_C), D ** -0.5),
        "gla_w_a2": nrm(ks[18], (N_ODD, 2, GLA_RANK, GLA_DK), GLA_RANK ** -0.5),
        "gla_b_a": nrm(ks[19], (N_ODD, 2, GLA_DK), 0.1),
        "gla_norm_g": 1.0 + nrm(ks[20], (N_ODD, GLA_HV), 0.02),
        "gla_w_out": nrm(ks[21], (N_ODD, GLA_DV, D), GLA_DV ** -0.5),
        "final_norm_g": 1.0 + nrm(ks[22], (D,), 0.02),
    }


def reference(x_prompt, x_sample, c_prompt, c_sample, mod_w, mod_b, norm_g, ffn_w_gate, ffn_w_up, ffn_w_down,
              ab_w_in, gmlp_ln_g, gmlp_ln_b, gmlp_w_s, gmlp_b_s, na_rpb, ab_w_out,
              gla_w_in, gla_w_a2, gla_b_a, gla_norm_g, gla_w_out, final_norm_g):
    y_prompt = encoder(x_prompt, c_prompt, mod_w, mod_b, norm_g, ffn_w_gate, ffn_w_up, ffn_w_down,
                       ab_w_in, gmlp_ln_g, gmlp_ln_b, gmlp_w_s, gmlp_b_s, na_rpb, ab_w_out,
                       gla_w_in, gla_w_a2, gla_b_a, gla_norm_g, gla_w_out, final_norm_g)
    y_sample = encoder(x_sample, c_sample, mod_w, mod_b, norm_g, ffn_w_gate, ffn_w_up, ffn_w_down,
                       ab_w_in, gmlp_ln_g, gmlp_ln_b, gmlp_w_s, gmlp_b_s, na_rpb, ab_w_out,
                       gla_w_in, gla_w_a2, gla_b_a, gla_norm_g, gla_w_out, final_norm_g)
    return (y_prompt, y_sample)
```

```python
import functools

import numpy as np
import jax
import jax.numpy as jnp
from jax import lax
from jax.experimental import pallas as pl
from jax.experimental.pallas import tpu as pltpu

F32 = jnp.float32
BF16 = jnp.bfloat16

NORM_EPS = 1e-6
N_MOD = 9
GRID_W = 64
GMLP_GROUPS = 8
GMLP_CHUNK = 128
NA_HEADS = 8
NA_KH = 8
NA_KW = 16
GLA_HEADS = 4
GLA_RANK = 16
GLA_TAU = 16.0
GLA_CHUNK = 64
MASK_VALUE = -1e30

LANES = 128
MIB = 2 ** 20

TM = 512
TF = 512
TN = 1024
NA_ROWS = 8
NA_WIN_ROWS = 16
GLA_BLOCK = 512


def _params(dims, vmem_mib):
    return pltpu.CompilerParams(dimension_semantics=dims, vmem_limit_bytes=vmem_mib * MIB)


def _grid_spec(n_prefetch, grid, in_specs, out_specs, scratch=()):
    return pltpu.PrefetchScalarGridSpec(num_scalar_prefetch=n_prefetch, grid=grid, in_specs=in_specs,
                                        out_specs=out_specs, scratch_shapes=list(scratch))


def _resident(shape):
    nd = len(shape)
    return pl.BlockSpec(shape, lambda *_: (0,) * nd, pipeline_mode=pl.Buffered(1))


def _norm_mod(x, g, shift, scale):
    y = x * lax.rsqrt(jnp.mean(x * x, axis=-1, keepdims=True) + NORM_EPS) * g
    return y * (1.0 + scale) + shift


def _silu(x):
    return x * jax.nn.sigmoid(x)


def _gelu(x):
    return 0.5 * x * (1.0 + lax.erf(x * np.float32(np.sqrt(0.5))))


def _mod_kernel(c_ref, w_ref, b_ref, o_ref):
    c = c_ref[...]
    o_ref[...] = jnp.dot(_silu(c).astype(BF16), w_ref[...].astype(BF16),
                         preferred_element_type=F32) + b_ref[...]


def _modulation(c, mod_w, mod_b):
    depth, d, n = mod_w.shape
    nseq = c.shape[0]
    rows = -(-nseq // 16) * 16
    c_pad = jnp.pad(c, ((0, rows - nseq), (0, 0)))
    tn = 1024
    out = pl.pallas_call(
        _mod_kernel,
        grid=(depth, n // tn),
        in_specs=[pl.BlockSpec((rows, d), lambda l, j: (0, 0)),
                  pl.BlockSpec((None, d, tn), lambda l, j: (l, 0, j)),
                  pl.BlockSpec((None, 1, tn), lambda l, j: (l, 0, j))],
        out_specs=pl.BlockSpec((None, rows, tn), lambda l, j: (l, 0, j)),
        out_shape=jax.ShapeDtypeStruct((depth, rows, n), F32),
        compiler_params=_params(("arbitrary", "arbitrary"), 40),
        name="modulation",
    )(c_pad, mod_w, mod_b.reshape(depth, 1, n))
    return out[:, :nseq].reshape(depth, nseq, N_MOD, d)


def _ffn_kernel(seq_ref, x_ref, mod_ref, g_ref, wg_ref, wu_ref, wd_ref, o_ref, h_ref, *, mod_base):
    j = pl.program_id(1)

    @pl.when(j == 0)
    def _():
        h = _norm_mod(x_ref[...], g_ref[...], mod_ref[mod_base:mod_base + 1, :],
                      mod_ref[mod_base + 1:mod_base + 2, :])
        h_ref[...] = h.astype(BF16)
        o_ref[...] = jnp.zeros_like(o_ref)

    h = h_ref[...]
    gate = jnp.dot(h, wg_ref[...], preferred_element_type=F32)
    up = jnp.dot(h, wu_ref[...], preferred_element_type=F32)
    a = (_silu(gate) * up).astype(BF16)
    o_ref[...] += jnp.dot(a, wd_ref[...], preferred_element_type=F32)

    @pl.when(j == pl.num_programs(1) - 1)
    def _():
        o_ref[...] = x_ref[...] + (0.5 * mod_ref[mod_base + 2:mod_base + 3, :]) * o_ref[...]


def _ffn(x, mod, seq_of_tile, norm_g, wg, wu, wd, mod_base):
    t, d = x.shape
    f = wg.shape[1]
    tok = lambda i, j, s: (i, 0)
    return pl.pallas_call(
        functools.partial(_ffn_kernel, mod_base=mod_base),
        grid_spec=_grid_spec(
            1, (t // TM, f // TF),
            [pl.BlockSpec((TM, d), tok),
             pl.BlockSpec((None, N_MOD, d), lambda i, j, s: (s[i], 0, 0)),
             pl.BlockSpec((1, d), lambda i, j, s: (0, 0)),
             pl.BlockSpec((d, TF), lambda i, j, s: (0, j)),
             pl.BlockSpec((d, TF), lambda i, j, s: (0, j)),
             pl.BlockSpec((TF, d), lambda i, j, s: (j, 0))],
            pl.BlockSpec((TM, d), tok),
            [pltpu.VMEM((TM, d), BF16)]),
        out_shape=jax.ShapeDtypeStruct((t, d), F32),
        compiler_params=_params(("parallel", "arbitrary"), 48),
        name="swiglu",
    )(seq_of_tile, x, mod, norm_g.reshape(1, d), wg, wu, wd)


def _ab_in_kernel(seq_ref, x_ref, mod_ref, g_ref, w_ref, lng_ref, lnb_ref, u_ref, z_ref, h_ref, *, q_scale):
    j = pl.program_id(1)

    @pl.when(j == 0)
    def _():
        h_ref[...] = _norm_mod(x_ref[...], g_ref[...], mod_ref[3:4, :], mod_ref[4:5, :]).astype(BF16)

    z = jnp.dot(h_ref[...], w_ref[...], preferred_element_type=F32)

    @pl.when(j == 0)
    def _():
        u_ref[...] = _gelu(z)

    @pl.when(j == 1)
    def _():
        v = _gelu(z)
        mu = jnp.mean(v, axis=-1, keepdims=True)
        vc = v - mu
        y = vc * lax.rsqrt(jnp.mean(vc * vc, axis=-1, keepdims=True) + NORM_EPS)
        z_ref[...] = (y * lng_ref[...] + lnb_ref[...]).astype(BF16)

    @pl.when(j == 2)
    def _():
        z_ref[...] = (z * q_scale).astype(BF16)

    @pl.when(j >= 3)
    def _():
        z_ref[...] = z.astype(BF16)


def _ab_in(x, mod, seq_of_tile, norm_g, w_in, ln_g, ln_b):
    t, d = x.shape
    n = w_in.shape[1]
    d_a = ln_g.shape[0]
    assert d_a == TN and n == 5 * TN
    head_dim = d_a // NA_HEADS
    return pl.pallas_call(
        functools.partial(_ab_in_kernel, q_scale=float(head_dim) ** -0.5),
        grid_spec=_grid_spec(
            1, (t // TM, n // TN),
            [pl.BlockSpec((TM, d), lambda i, j, s: (i, 0)),
             pl.BlockSpec((None, N_MOD, d), lambda i, j, s: (s[i], 0, 0)),
             pl.BlockSpec((1, d), lambda i, j, s: (0, 0)),
             pl.BlockSpec((d, TN), lambda i, j, s: (0, j)),
             pl.BlockSpec((1, d_a), lambda i, j, s: (0, 0)),
             pl.BlockSpec((1, d_a), lambda i, j, s: (0, 0))],
            [pl.BlockSpec((TM, TN), lambda i, j, s: (i, 0)),
             pl.BlockSpec((TM, TN), lambda i, j, s: (i, jnp.maximum(j - 1, 0)))],
            [pltpu.VMEM((TM, d), BF16)]),
        out_shape=[jax.ShapeDtypeStruct((t, TN), F32), jax.ShapeDtypeStruct((t, n - TN), BF16)],
        compiler_params=_params(("parallel", "arbitrary"), 40),
        name="ab_in_proj",
    )(seq_of_tile, x, mod, norm_g.reshape(1, d), w_in, ln_g.reshape(1, d_a), ln_b.reshape(1, d_a))


def _na_kernel(lb_ref, rows_ref, win_ref, q_ref, k_ref, v_ref, bias_ref, o_ref):
    i = pl.program_id(0)
    lb = lb_ref[i]
    rows = rows_ref[i]
    head_dim = q_ref.shape[1] // NA_HEADS
    band = NA_KH * GRID_W
    win_row0 = jnp.clip(lb * NA_ROWS - NA_KH // 2, 0, rows - NA_WIN_ROWS)

    def row_body(rl, carry):
        r = lb * NA_ROWS + rl
        row_start = jnp.clip(r - NA_KH // 2, 0, rows - NA_KH)
        off = pl.multiple_of((row_start - win_row0) * GRID_W, GRID_W)
        shift = row_start - r + (NA_KH - 1)
        q_rows = pl.ds(pl.multiple_of(rl * GRID_W, GRID_W), GRID_W)
        for h in range(NA_HEADS):
            cols = slice(h * head_dim, (h + 1) * head_dim)
            q = q_ref[q_rows, cols]
            kb = k_ref[pl.ds(off, band), cols]
            vb = v_ref[pl.ds(off, band), cols]
            s = lax.dot_general(q, kb, (((1,), (1,)), ((), ())), preferred_element_type=F32)
            s = s + bias_ref[h, pl.ds(shift, 1)][0]
            m = jnp.max(s, axis=-1, keepdims=True)
            p = jnp.exp(s - m)
            l = jnp.sum(p, axis=-1, keepdims=True)
            o = jnp.dot(p.astype(BF16), vb, preferred_element_type=F32) / l
            o_ref[q_rows, cols] = o.astype(BF16)
        return carry

    lax.fori_loop(0, NA_ROWS, row_body, 0)


def _na_bias_table(rpb):
    qc = np.arange(GRID_W)
    kc = np.arange(GRID_W)
    cstart = np.clip(qc - NA_KW // 2, 0, GRID_W - NA_KW)
    valid = (kc[None, :] >= cstart[:, None]) & (kc[None, :] < cstart[:, None] + NA_KW)
    col_idx = np.clip(kc[None, :] - qc[:, None] + NA_KW - 1, 0, 2 * NA_KW - 2)
    row_idx = np.arange(NA_KH)[:, None] + np.arange(NA_KH)[None, :]
    t = rpb.astype(F32)[:, row_idx[:, None, :, None], col_idx[None, :, None, :]]
    t = jnp.where(valid[None, None, :, None, :], t, F32(MASK_VALUE))
    return t.reshape(rpb.shape[0], NA_KH, GRID_W, NA_KH * GRID_W)


def _na(z, bias, seq_lens):
    t = z.shape[0]
    d_b = z.shape[1] // 4
    blk = NA_ROWS * GRID_W
    win = NA_WIN_ROWS * GRID_W
    lb, rows, wstart = [], [], []
    base = 0
    for n in seq_lens:
        assert n % blk == 0 and n >= win
        for b in range(n // blk):
            lb.append(b)
            rows.append(n // GRID_W)
            wstart.append((base + int(np.clip(b * blk - (NA_KH // 2) * GRID_W, 0, n - win))) // GRID_W)
        base += n
    tables = [jnp.asarray(np.array(a, np.int32)) for a in (lb, rows, wstart)]
    return pl.pallas_call(
        _na_kernel,
        grid_spec=_grid_spec(
            3, (t // blk,),
            [pl.BlockSpec((blk, d_b), lambda i, lb, rows, ws: (i, 1)),
             pl.BlockSpec((pl.Element(win), pl.Element(d_b)), lambda i, lb, rows, ws: (ws[i] * GRID_W, 2 * d_b)),
             pl.BlockSpec((pl.Element(win), pl.Element(d_b)), lambda i, lb, rows, ws: (ws[i] * GRID_W, 3 * d_b)),
             _resident(bias.shape)],
            pl.BlockSpec((blk, d_b), lambda i, lb, rows, ws: (i, 0))),
        out_shape=jax.ShapeDtypeStruct((t, d_b), BF16),
        compiler_params=_params(("parallel",), 32),
        name="neighborhood_attention",
    )(*tables, z, z, z, bias)


def _ab_out_kernel(seq_ref, u_ref, z_ref, nb_ref, ws_ref, bs_ref, wo_ref, x_ref, mod_ref, o_ref):
    tm, d_a = u_ref.shape
    nch = tm // GMLP_CHUNK
    cg = d_a // GMLP_GROUPS
    vs = z_ref[...]
    s_blocks = [[None] * GMLP_GROUPS for _ in range(nch)]
    for g in range(GMLP_GROUPS):
        cat = jnp.concatenate([vs[n * GMLP_CHUNK:(n + 1) * GMLP_CHUNK, g * cg:(g + 1) * cg] for n in range(nch)],
                              axis=1)
        sg = jnp.dot(ws_ref[g], cat, preferred_element_type=F32)
        for n in range(nch):
            s_blocks[n][g] = sg[:, n * cg:(n + 1) * cg] + bs_ref[g]
    s = jnp.concatenate([jnp.concatenate(row, axis=1) for row in s_blocks], axis=0)
    a = (u_ref[...] * s).astype(BF16)
    y = (jnp.dot(a, wo_ref[0:d_a, :], preferred_element_type=F32)
         + jnp.dot(nb_ref[...], wo_ref[d_a:, :], preferred_element_type=F32))
    o_ref[...] = x_ref[...] + mod_ref[5:6, :] * y


def _ab_out(x, mod, seq_of_tile, u, z, nb, w_s, b_s, w_out):
    t, d = x.shape
    d_a = u.shape[1]
    cg = d_a // GMLP_GROUPS
    assert cg == LANES and GMLP_CHUNK == LANES
    bs_tile = jnp.broadcast_to(b_s.astype(F32)[:, :, None], (GMLP_GROUPS, GMLP_CHUNK, cg))
    return pl.pallas_call(
        _ab_out_kernel,
        grid_spec=_grid_spec(
            1, (t // TM,),
            [pl.BlockSpec((TM, d_a), lambda i, s: (i, 0)),
             pl.BlockSpec((TM, d_a), lambda i, s: (i, 0)),
             pl.BlockSpec((TM, nb.shape[1]), lambda i, s: (i, 0)),
             _resident(w_s.shape),
             _resident(bs_tile.shape),
             _resident(w_out.shape),
             pl.BlockSpec((TM, d), lambda i, s: (i, 0)),
             pl.BlockSpec((None, N_MOD, d), lambda i, s: (s[i], 0, 0))],
            pl.BlockSpec((TM, d), lambda i, s: (i, 0))),
        out_shape=jax.ShapeDtypeStruct((t, d), F32),
        compiler_params=_params(("parallel",), 48),
        name="ab_out_proj",
    )(seq_of_tile, u, z, nb, w_s, bs_tile, w_out, x, mod)


def _gla_in_kernel(seq_ref, x_ref, mod_ref, g_ref, w_ref, wa_ref, zf_ref, zv_ref, a_ref, h_ref, *, q_scale, n_f32):
    j = pl.program_id(1)

    @pl.when(j == 0)
    def _():
        h = _norm_mod(x_ref[...], g_ref[...], mod_ref[3:4, :], mod_ref[4:5, :]).astype(BF16)
        h_ref[...] = h
        a_ref[...] = jnp.dot(h, wa_ref[...], preferred_element_type=F32)

    z = jnp.dot(h_ref[...], w_ref[...], preferred_element_type=F32)

    @pl.when(j == 0)
    def _():
        zf_ref[...] = z * q_scale

    @pl.when(jnp.logical_and(j > 0, j < n_f32))
    def _():
        zf_ref[...] = z

    @pl.when(j >= n_f32)
    def _():
        zv_ref[...] = z.astype(BF16)


def _gla_in(x, mod, seq_of_tile, norm_g, w_main, w_a, dk, dv):
    t, d = x.shape
    assert dk == TN and dv % TN == 0
    n_f32 = (2 * dk + dv) // TN
    n_all = w_main.shape[1] // TN
    hk = dk // GLA_HEADS
    return pl.pallas_call(
        functools.partial(_gla_in_kernel, q_scale=float(hk) ** -0.5, n_f32=n_f32),
        grid_spec=_grid_spec(
            1, (t // TM, n_all),
            [pl.BlockSpec((TM, d), lambda i, j, s: (i, 0)),
             pl.BlockSpec((None, N_MOD, d), lambda i, j, s: (s[i], 0, 0)),
             pl.BlockSpec((1, d), lambda i, j, s: (0, 0)),
             pl.BlockSpec((d, TN), lambda i, j, s: (0, j)),
             _resident(w_a.shape)],
            [pl.BlockSpec((TM, TN), lambda i, j, s: (i, jnp.minimum(j, n_f32 - 1))),
             pl.BlockSpec((TM, TN), lambda i, j, s: (i, jnp.maximum(j - n_f32, 0))),
             pl.BlockSpec((TM, LANES), lambda i, j, s: (i, 0))],
            [pltpu.VMEM((TM, d), BF16)]),
        out_shape=[jax.ShapeDtypeStruct((t, 2 * dk + dv), F32), jax.ShapeDtypeStruct((t, dv), BF16),
                   jax.ShapeDtypeStruct((t, LANES), F32)],
        compiler_params=_params(("parallel", "arbitrary"), 40),
        name="gla_in_proj",
    )(seq_of_tile, x, mod, norm_g.reshape(1, d), w_main, w_a)


def _log_sigmoid(x):
    return jnp.minimum(x, 0.0) - jnp.log1p(jnp.exp(-jnp.abs(x)))


def _gla_sweep_kernel(order_ref, first_ref, q_ref, k_ref, v_ref, a_ref, wa2_ref, ba_ref, *rest, reverse):
    if reverse:
        prev_ref, o_ref, st_ref = rest
    else:
        o_ref, st_ref = rest
    i = pl.program_id(0)
    L = GLA_CHUNK
    tc, dk = q_ref.shape
    dv = v_ref.shape[1]
    hk, hv = dk // GLA_HEADS, dv // GLA_HEADS

    @pl.when(first_ref[i] == 1)
    def _():
        st_ref[...] = jnp.zeros_like(st_ref)

    gate_pre = jnp.dot(a_ref[...].astype(BF16), wa2_ref[...], preferred_element_type=F32) + ba_ref[...]
    g_all = _log_sigmoid(gate_pre) * np.float32(1.0 / GLA_TAU)

    row = lax.broadcasted_iota(jnp.int32, (L, L), 0)
    col = lax.broadcasted_iota(jnp.int32, (L, L), 1)
    if reverse:
        tri = (col >= row).astype(BF16)
        keep = col > row
        i_ref, i_last = L - 1 - L // 2, 0
    else:
        tri = (col <= row).astype(BF16)
        keep = col <= row
        i_ref, i_last = L // 2, L - 1

    nchunks = tc // L
    for c in (range(nchunks - 1, -1, -1) if reverse else range(nchunks)):
        rs = slice(c * L, (c + 1) * L)
        gc = g_all[rs, :]
        g1 = gc.astype(BF16)
        r1 = gc - g1.astype(F32)
        g2 = r1.astype(BF16)
        g3 = (r1 - g2.astype(F32)).astype(BF16)
        b = (jnp.dot(tri, g1, preferred_element_type=F32) + jnp.dot(tri, g2, preferred_element_type=F32)
             + jnp.dot(tri, g3, preferred_element_type=F32))
        b_mid = b[i_ref:i_ref + 1, :]
        b_last = b[i_last:i_last + 1, :]
        qc = q_ref[rs, :]
        kc = k_ref[rs, :]
        qa = (qc * jnp.exp(b - b_mid)).astype(BF16)
        ka = (kc * jnp.exp(b_mid - b)).astype(BF16)
        qx = (qc * jnp.exp(b)).astype(BF16)
        kx = (kc * jnp.exp(b_last - b)).astype(BF16)
        decay = jnp.exp(b_last)
        vc = v_ref[rs, :]
        for h in range(GLA_HEADS):
            ks = slice(h * hk, (h + 1) * hk)
            vs = slice(h * hv, (h + 1) * hv)
            att = lax.dot_general(qa[:, ks], ka[:, ks], (((1,), (1,)), ((), ())), preferred_element_type=F32)
            att = jnp.where(keep, att, 0.0).astype(BF16)
            st = st_ref[h]
            o = (jnp.dot(att, vc[:, vs], preferred_element_type=F32)
                 + lax.dot_general(qx[:, ks], st.astype(BF16), (((1,), (1,)), ((), ())),
                                   preferred_element_type=F32))
            st_ref[h] = st * decay[:, ks] + lax.dot_general(vc[:, vs], kx[:, ks], (((0,), (0,)), ((), ())),
                                                            preferred_element_type=F32)
            if reverse:
                o = o + prev_ref[rs, vs]
            o_ref[rs, vs] = o


def _gla_sweep(zf, zv, a, wa2, ba, seq_lens, reverse, prev=None):
    t, dv = zv.shape
    dk = wa2.shape[1]
    tc = GLA_BLOCK
    nblk = t // tc
    first = np.zeros(nblk, np.int32)
    base = 0
    for n in seq_lens:
        assert n % tc == 0
        first[(base + n) // tc - 1 if reverse else base // tc] = 1
        base += n
    order = np.arange(nblk, dtype=np.int32)
    if reverse:
        order = order[::-1].copy()
        first = first[order]
    tok = lambda col: (lambda i, order, first: (order[i], col))
    in_specs = [pl.BlockSpec((tc, dk), tok(0)),
                pl.BlockSpec((tc, dk), tok(1)),
                pl.BlockSpec((tc, dv), tok(0)),
                pl.BlockSpec((tc, LANES), tok(0)),
                _resident(wa2.shape),
                _resident(ba.shape)]
    args = [zf, zf, zv, a, wa2, ba]
    if reverse:
        in_specs.append(pl.BlockSpec((tc, dv), tok(0)))
        args.append(prev)
    return pl.pallas_call(
        functools.partial(_gla_sweep_kernel, reverse=reverse),
        grid_spec=_grid_spec(
            2, (nblk,), in_specs, pl.BlockSpec((tc, dv), tok(0)),
            [pltpu.VMEM((GLA_HEADS, dv // GLA_HEADS, dk // GLA_HEADS), F32)]),
        out_shape=jax.ShapeDtypeStruct((t, dv), F32),
        compiler_params=_params(("arbitrary",), 48),
        name="gla_sweep_bwd" if reverse else "gla_sweep_fwd",
    )(jnp.asarray(order), jnp.asarray(first), *args)


def _gla_out_kernel(seq_ref, o_ref, r_ref, ng_ref, wo_ref, x_ref, mod_ref, out_ref):
    dv = o_ref.shape[1]
    hv = dv // GLA_HEADS
    o = o_ref[...]
    parts = []
    for h in range(GLA_HEADS):
        oh = o[:, h * hv:(h + 1) * hv]
        oh = oh * lax.rsqrt(jnp.mean(oh * oh, axis=-1, keepdims=True) + NORM_EPS)
        parts.append(oh * ng_ref[...])
    y = (jnp.concatenate(parts, axis=1) * _silu(r_ref[...])).astype(BF16)
    out_ref[...] = x_ref[...] + mod_ref[5:6, :] * jnp.dot(y, wo_ref[...], preferred_element_type=F32)


def _gla_out(x, mod, seq_of_tile, o, zf, norm_g, w_out, tm):
    t, d = x.shape
    dv = o.shape[1]
    r_col = (zf.shape[1] - dv) // dv
    assert r_col * dv + dv == zf.shape[1]
    return pl.pallas_call(
        _gla_out_kernel,
        grid_spec=_grid_spec(
            1, (t // tm,),
            [pl.BlockSpec((tm, dv), lambda i, s: (i, 0)),
             pl.BlockSpec((tm, dv), lambda i, s: (i, r_col)),
             pl.BlockSpec((1, dv // GLA_HEADS), lambda i, s: (0, 0)),
             _resident(w_out.shape),
             pl.BlockSpec((tm, d), lambda i, s: (i, 0)),
             pl.BlockSpec((None, N_MOD, d), lambda i, s: (s[i], 0, 0))],
            pl.BlockSpec((tm, d), lambda i, s: (i, 0))),
        out_shape=jax.ShapeDtypeStruct((t, d), F32),
        compiler_params=_params(("parallel",), 48),
        name="gla_out_proj",
    )(seq_of_tile, o, zf, norm_g.reshape(1, -1), w_out, x, mod)


def _final_norm_kernel(x_ref, g_ref, o_ref):
    x = x_ref[...]
    o_ref[...] = x * lax.rsqrt(jnp.mean(x * x, axis=-1, keepdims=True) + NORM_EPS) * g_ref[...]


def _final_norm(x, g):
    t, d = x.shape
    return pl.pallas_call(
        _final_norm_kernel,
        grid=(t // TM,),
        in_specs=[pl.BlockSpec((TM, d), lambda i: (i, 0)), pl.BlockSpec((1, d), lambda i: (0, 0))],
        out_specs=pl.BlockSpec((TM, d), lambda i: (i, 0)),
        out_shape=jax.ShapeDtypeStruct((t, d), F32),
        compiler_params=_params(("parallel",), 32),
        name="final_norm",
    )(x, g.reshape(1, d))


def _seq_table(seq_lens, tile):
    out = []
    for s, n in enumerate(seq_lens):
        assert n % tile == 0
        out += [s] * (n // tile)
    return jnp.asarray(np.array(out, np.int32))


def kernel(x_prompt, x_sample, c_prompt, c_sample, mod_w, mod_b, norm_g, ffn_w_gate, ffn_w_up, ffn_w_down,
           ab_w_in, gmlp_ln_g, gmlp_ln_b, gmlp_w_s, gmlp_b_s, na_rpb, ab_w_out,
           gla_w_in, gla_w_a2, gla_b_a, gla_norm_g, gla_w_out, final_norm_g):
    d = x_prompt.shape[-1]
    depth = mod_w.shape[0]
    seq_lens = (x_prompt.shape[1],) * x_prompt.shape[0] + (x_sample.shape[1],) * x_sample.shape[0]
    n_prompt = x_prompt.shape[0] * x_prompt.shape[1]

    x = jnp.concatenate([x_prompt.reshape(-1, d), x_sample.reshape(-1, d)], axis=0)
    c = jnp.concatenate([c_prompt, c_sample], axis=0)
    mod = _modulation(c, mod_w, mod_b)
    seq_tm = _seq_table(seq_lens, TM)
    gla_tm = TM // 2
    seq_gla = _seq_table(seq_lens, gla_tm)

    dk = gla_w_a2.shape[-1]
    dv = gla_w_out.shape[1]

    for i in range(depth):
        m = mod[i]
        x = _ffn(x, m, seq_tm, norm_g[i, 0], ffn_w_gate[i, 0].astype(BF16), ffn_w_up[i, 0].astype(BF16),
                 ffn_w_down[i, 0].astype(BF16), 0)
        j = i // 2
        if i % 2 == 0:
            u, z = _ab_in(x, m, seq_tm, norm_g[i, 1], ab_w_in[j].astype(BF16), gmlp_ln_g[j], gmlp_ln_b[j])
            nb = _na(z, _na_bias_table(na_rpb[j]), seq_lens)
            x = _ab_out(x, m, seq_tm, u, z, nb, gmlp_w_s[j].astype(BF16), gmlp_b_s[j], ab_w_out[j].astype(BF16))
        else:
            w = gla_w_in[j]
            w_main = jnp.concatenate([w[:, :2 * dk], w[:, 2 * dk + dv:2 * dk + 2 * dv], w[:, 2 * dk:2 * dk + dv]],
                                     axis=1).astype(BF16)
            w_a = jnp.pad(w[:, 2 * dk + 2 * dv:], ((0, 0), (0, LANES - 2 * GLA_RANK))).astype(BF16)
            zf, zv, a = _gla_in(x, m, seq_tm, norm_g[i, 1], w_main, w_a, dk, dv)
            outs = None
            for z_dir in range(2):
                wa2 = jnp.zeros((LANES, dk), F32).at[z_dir * GLA_RANK:(z_dir + 1) * GLA_RANK].set(gla_w_a2[j, z_dir])
                outs = _gla_sweep(zf, zv, a, wa2.astype(BF16), gla_b_a[j, z_dir].reshape(1, dk).astype(F32),
                                  seq_lens, reverse=bool(z_dir), prev=outs)
            x = _gla_out(x, m, seq_gla, outs, zf, gla_norm_g[j], gla_w_out[j].astype(BF16), gla_tm)
        x = _ffn(x, m, seq_tm, norm_g[i, 2], ffn_w_gate[i, 1].astype(BF16), ffn_w_up[i, 1].astype(BF16),
                 ffn_w_down[i, 1].astype(BF16), 6)
    y = _final_norm(x, final_norm_g)
    return (y[:n_prompt].reshape(x_prompt.shape), y[n_prompt:].reshape(x_sample.shape))
```

```python
import functools

import numpy as np
import jax
import jax.numpy as jnp
from jax import lax
from jax.experimental import pallas as pl
from jax.experimental.pallas import tpu as pltpu

F32 = jnp.float32
BF16 = jnp.bfloat16

NORM_EPS = 1e-6
N_MOD = 9
GRID_W = 64
GMLP_GROUPS = 8
GMLP_CHUNK = 128
NA_HEADS = 8
NA_KH = 8
NA_KW = 16
GLA_HEADS = 4
GLA_RANK = 16
GLA_TAU = 16.0
GLA_CHUNK = 64
MASK_VALUE = -1e30

LANES = 128
MIB = 2 ** 20

TM = 512
TM_FFN = 512
TF = 512
TN = 1024
NA_ROWS = 8
NA_WIN_ROWS = 16
GLA_BLOCK = 512


def _params(dims, vmem_mib):
    return pltpu.CompilerParams(dimension_semantics=dims, vmem_limit_bytes=vmem_mib * MIB)


def _grid_spec(n_prefetch, grid, in_specs, out_specs, scratch=()):
    return pltpu.PrefetchScalarGridSpec(num_scalar_prefetch=n_prefetch, grid=grid, in_specs=in_specs,
                                        out_specs=out_specs, scratch_shapes=list(scratch))


def _resident(shape):
    nd = len(shape)
    return pl.BlockSpec(shape, lambda *_: (0,) * nd, pipeline_mode=pl.Buffered(1))


def _norm_mod(x, g, shift, scale):
    y = x * lax.rsqrt(jnp.mean(x * x, axis=-1, keepdims=True) + NORM_EPS) * g
    return y * (1.0 + scale) + shift


def _silu(x):
    return x * jax.nn.sigmoid(x)


def _gelu(x):
    return 0.5 * x * (1.0 + lax.erf(x * np.float32(np.sqrt(0.5))))


def _mod_kernel(c_ref, w_ref, b_ref, o_ref):
    c = c_ref[...]
    o_ref[...] = jnp.dot(_silu(c).astype(BF16), w_ref[...].astype(BF16),
                         preferred_element_type=F32) + b_ref[...]


def _modulation(c, mod_w, mod_b):
    depth, d, n = mod_w.shape
    nseq = c.shape[0]
    rows = -(-nseq // 16) * 16
    c_pad = jnp.pad(c, ((0, rows - nseq), (0, 0)))
    tn = 1024
    out = pl.pallas_call(
        _mod_kernel,
        grid=(depth, n // tn),
        in_specs=[pl.BlockSpec((rows, d), lambda l, j: (0, 0)),
                  pl.BlockSpec((None, d, tn), lambda l, j: (l, 0, j)),
                  pl.BlockSpec((None, 1, tn), lambda l, j: (l, 0, j))],
        out_specs=pl.BlockSpec((None, rows, tn), lambda l, j: (l, 0, j)),
        out_shape=jax.ShapeDtypeStruct((depth, rows, n), F32),
        compiler_params=_params(("arbitrary", "arbitrary"), 40),
        name="modulation",
    )(c_pad, mod_w, mod_b.reshape(depth, 1, n))
    return out[:, :nseq].reshape(depth, nseq, N_MOD, d)


def _ffn_kernel(seq_ref, x_ref, mod_ref, g_ref, wg_ref, wu_ref, wd_ref, o_ref, h_ref, *, mod_base):
    j = pl.program_id(1)

    @pl.when(j == 0)
    def _():
        h = _norm_mod(x_ref[...], g_ref[...], mod_ref[mod_base:mod_base + 1, :],
                      mod_ref[mod_base + 1:mod_base + 2, :])
        h_ref[...] = h.astype(BF16)
        o_ref[...] = x_ref[...]

    h = h_ref[...]
    gate = jnp.dot(h, wg_ref[...], preferred_element_type=F32)
    up = jnp.dot(h, wu_ref[...], preferred_element_type=F32)
    a = (_silu(gate) * up).astype(BF16)
    o_ref[...] += (0.5 * mod_ref[mod_base + 2:mod_base + 3, :]) * jnp.dot(a, wd_ref[...],
                                                                           preferred_element_type=F32)


def _ffn(x, mod, seq_of_tile, norm_g, wg, wu, wd, mod_base):
    t, d = x.shape
    f = wg.shape[1]
    tok = lambda i, j, s: (i, 0)
    return pl.pallas_call(
        functools.partial(_ffn_kernel, mod_base=mod_base),
        grid_spec=_grid_spec(
            1, (t // TM_FFN, f // TF),
            [pl.BlockSpec((TM_FFN, d), tok),
             pl.BlockSpec((None, N_MOD, d), lambda i, j, s: (s[i], 0, 0)),
             pl.BlockSpec((1, d), lambda i, j, s: (0, 0)),
             pl.BlockSpec((d, TF), lambda i, j, s: (0, j)),
             pl.BlockSpec((d, TF), lambda i, j, s: (0, j)),
             pl.BlockSpec((TF, d), lambda i, j, s: (j, 0))],
            pl.BlockSpec((TM_FFN, d), tok),
            [pltpu.VMEM((TM_FFN, d), BF16)]),
        out_shape=jax.ShapeDtypeStruct((t, d), F32),
        compiler_params=_params(("parallel", "arbitrary"), 48),
        name="swiglu",
    )(seq_of_tile, x, mod, norm_g.reshape(1, d), wg, wu, wd)


def _ab_in_kernel(seq_ref, x_ref, mod_ref, g_ref, w_ref, lng_ref, lnb_ref, u_ref, z_ref, *, q_scale):
    h = _norm_mod(x_ref[...], g_ref[...], mod_ref[3:4, :], mod_ref[4:5, :]).astype(BF16)

    def col(j):
        return jnp.dot(h, w_ref[:, j * TN:(j + 1) * TN], preferred_element_type=F32)

    u_ref[...] = _gelu(col(0))
    v = _gelu(col(1))
    vc = v - jnp.mean(v, axis=-1, keepdims=True)
    y = vc * lax.rsqrt(jnp.mean(vc * vc, axis=-1, keepdims=True) + NORM_EPS)
    z_ref[:, 0:TN] = (y * lng_ref[...] + lnb_ref[...]).astype(BF16)
    z_ref[:, TN:2 * TN] = (col(2) * q_scale).astype(BF16)
    z_ref[:, 2 * TN:3 * TN] = col(3).astype(BF16)
    z_ref[:, 3 * TN:4 * TN] = col(4).astype(BF16)


def _ab_in(x, mod, seq_of_tile, norm_g, w_in, ln_g, ln_b):
    t, d = x.shape
    n = w_in.shape[1]
    d_a = ln_g.shape[0]
    assert d_a == TN and n == 5 * TN
    head_dim = d_a // NA_HEADS
    return pl.pallas_call(
        functools.partial(_ab_in_kernel, q_scale=float(head_dim) ** -0.5),
        grid_spec=_grid_spec(
            1, (t // TM,),
            [pl.BlockSpec((TM, d), lambda i, s: (i, 0)),
             pl.BlockSpec((None, N_MOD, d), lambda i, s: (s[i], 0, 0)),
             pl.BlockSpec((1, d), lambda i, s: (0, 0)),
             _resident(w_in.shape),
             pl.BlockSpec((1, d_a), lambda i, s: (0, 0)),
             pl.BlockSpec((1, d_a), lambda i, s: (0, 0))],
            [pl.BlockSpec((TM, TN), lambda i, s: (i, 0)),
             pl.BlockSpec((TM, n - TN), lambda i, s: (i, 0))]),
        out_shape=[jax.ShapeDtypeStruct((t, TN), F32), jax.ShapeDtypeStruct((t, n - TN), BF16)],
        compiler_params=_params(("parallel",), 52),
        name="ab_in_proj",
    )(seq_of_tile, x, mod, norm_g.reshape(1, d), w_in, ln_g.reshape(1, d_a), ln_b.reshape(1, d_a))


def _na_kernel(lb_ref, rows_ref, win_ref, q_ref, k_ref, v_ref, bias_ref, o_ref):
    i = pl.program_id(0)
    lb = lb_ref[i]
    rows = rows_ref[i]
    head_dim = q_ref.shape[1] // NA_HEADS
    band = NA_KH * GRID_W
    win_row0 = jnp.clip(lb * NA_ROWS - NA_KH // 2, 0, rows - NA_WIN_ROWS)

    def row_body(rl, carry):
        r = lb * NA_ROWS + rl
        row_start = jnp.clip(r - NA_KH // 2, 0, rows - NA_KH)
        off = pl.multiple_of((row_start - win_row0) * GRID_W, GRID_W)
        shift = row_start - r + (NA_KH - 1)
        q_rows = pl.ds(pl.multiple_of(rl * GRID_W, GRID_W), GRID_W)
        heads = [slice(h * head_dim, (h + 1) * head_dim) for h in range(NA_HEADS)]
        scores = [lax.dot_general(q_ref[q_rows, cols], k_ref[pl.ds(off, band), cols], (((1,), (1,)), ((), ())),
                                  preferred_element_type=F32) for cols in heads]
        probs, norms = [], []
        for h in range(NA_HEADS):
            s = scores[h] + bias_ref[h, pl.ds(shift, 1)][0]
            p = jnp.exp(s - jnp.max(s, axis=-1, keepdims=True))
            norms.append(jnp.sum(p, axis=-1, keepdims=True))
            probs.append(p.astype(BF16))
        for h, cols in enumerate(heads):
            o = jnp.dot(probs[h], v_ref[pl.ds(off, band), cols], preferred_element_type=F32) / norms[h]
            o_ref[q_rows, cols] = o.astype(BF16)
        return carry

    lax.fori_loop(0, NA_ROWS, row_body, 0)


def _na_bias_table(rpb):
    qc = np.arange(GRID_W)
    kc = np.arange(GRID_W)
    cstart = np.clip(qc - NA_KW // 2, 0, GRID_W - NA_KW)
    valid = (kc[None, :] >= cstart[:, None]) & (kc[None, :] < cstart[:, None] + NA_KW)
    col_idx = np.clip(kc[None, :] - qc[:, None] + NA_KW - 1, 0, 2 * NA_KW - 2)
    onehot = (col_idx[None] == np.arange(2 * NA_KW - 1)[:, None, None]).astype(np.float32)
    cols = jnp.einsum('hij,jqk->hiqk', rpb.astype(F32), onehot, precision=lax.Precision.HIGHEST)
    cols = jnp.where(valid[None, None], cols, F32(MASK_VALUE))
    t = jnp.stack([cols[:, s:s + NA_KH] for s in range(NA_KH)], axis=1)
    t = jnp.transpose(t, (0, 1, 3, 2, 4))
    return t.reshape(rpb.shape[0], NA_KH, GRID_W, NA_KH * GRID_W)


def _na(z, bias, seq_lens):
    t = z.shape[0]
    d_b = z.shape[1] // 4
    blk = NA_ROWS * GRID_W
    win = NA_WIN_ROWS * GRID_W
    lb, rows, wstart = [], [], []
    base = 0
    for n in seq_lens:
        assert n % blk == 0 and n >= win
        for b in range(n // blk):
            lb.append(b)
            rows.append(n // GRID_W)
            wstart.append((base + int(np.clip(b * blk - (NA_KH // 2) * GRID_W, 0, n - win))) // GRID_W)
        base += n
    tables = [jnp.asarray(np.array(a, np.int32)) for a in (lb, rows, wstart)]
    return pl.pallas_call(
        _na_kernel,
        grid_spec=_grid_spec(
            3, (t // blk,),
            [pl.BlockSpec((blk, d_b), lambda i, lb, rows, ws: (i, 1)),
             pl.BlockSpec((pl.Element(win), pl.Element(d_b)), lambda i, lb, rows, ws: (ws[i] * GRID_W, 2 * d_b)),
             pl.BlockSpec((pl.Element(win), pl.Element(d_b)), lambda i, lb, rows, ws: (ws[i] * GRID_W, 3 * d_b)),
             _resident(bias.shape)],
            pl.BlockSpec((blk, d_b), lambda i, lb, rows, ws: (i, 0))),
        out_shape=jax.ShapeDtypeStruct((t, d_b), BF16),
        compiler_params=_params(("parallel",), 32),
        name="neighborhood_attention",
    )(*tables, z, z, z, bias)


def _ab_out_kernel(seq_ref, u_ref, z_ref, nb_ref, ws_ref, bs_ref, wo_ref, x_ref, mod_ref, o_ref):
    tm, d_a = u_ref.shape
    nch = tm // GMLP_CHUNK
    cg = d_a // GMLP_GROUPS
    vs = z_ref[...]
    s_blocks = [[None] * GMLP_GROUPS for _ in range(nch)]
    for g in range(GMLP_GROUPS):
        cat = jnp.concatenate([vs[n * GMLP_CHUNK:(n + 1) * GMLP_CHUNK, g * cg:(g + 1) * cg] for n in range(nch)],
                              axis=1)
        sg = jnp.dot(ws_ref[g], cat, preferred_element_type=F32)
        for n in range(nch):
            s_blocks[n][g] = sg[:, n * cg:(n + 1) * cg] + bs_ref[g]
    s = jnp.concatenate([jnp.concatenate(row, axis=1) for row in s_blocks], axis=0)
    a = (u_ref[...] * s).astype(BF16)
    y = (jnp.dot(a, wo_ref[0:d_a, :], preferred_element_type=F32)
         + jnp.dot(nb_ref[...], wo_ref[d_a:, :], preferred_element_type=F32))
    o_ref[...] = x_ref[...] + mod_ref[5:6, :] * y


def _ab_out(x, mod, seq_of_tile, u, z, nb, w_s, b_s, w_out):
    t, d = x.shape
    d_a = u.shape[1]
    cg = d_a // GMLP_GROUPS
    assert cg == LANES and GMLP_CHUNK == LANES
    bs_tile = jnp.broadcast_to(b_s.astype(F32)[:, :, None], (GMLP_GROUPS, GMLP_CHUNK, cg))
    return pl.pallas_call(
        _ab_out_kernel,
        grid_spec=_grid_spec(
            1, (t // TM,),
            [pl.BlockSpec((TM, d_a), lambda i, s: (i, 0)),
             pl.BlockSpec((TM, d_a), lambda i, s: (i, 0)),
             pl.BlockSpec((TM, nb.shape[1]), lambda i, s: (i, 0)),
             _resident(w_s.shape),
             _resident(bs_tile.shape),
             _resident(w_out.shape),
             pl.BlockSpec((TM, d), lambda i, s: (i, 0)),
             pl.BlockSpec((None, N_MOD, d), lambda i, s: (s[i], 0, 0))],
            pl.BlockSpec((TM, d), lambda i, s: (i, 0))),
        out_shape=jax.ShapeDtypeStruct((t, d), F32),
        compiler_params=_params(("parallel",), 48),
        name="ab_out_proj",
    )(seq_of_tile, u, z, nb, w_s, bs_tile, w_out, x, mod)


def _gla_in_kernel(seq_ref, x_ref, mod_ref, g_ref, w_ref, wa_ref, zf_ref, zv_ref, a_ref, *, q_scale, n_f32):
    h = _norm_mod(x_ref[...], g_ref[...], mod_ref[3:4, :], mod_ref[4:5, :]).astype(BF16)
    a_ref[...] = jnp.dot(h, wa_ref[...], preferred_element_type=F32)
    for j in range(w_ref.shape[1] // TN):
        z = jnp.dot(h, w_ref[:, j * TN:(j + 1) * TN], preferred_element_type=F32)
        if j == 0:
            z = z * q_scale
        if j < n_f32:
            zf_ref[:, j * TN:(j + 1) * TN] = z
        else:
            zv_ref[:, (j - n_f32) * TN:(j - n_f32 + 1) * TN] = z.astype(BF16)


def _gla_in(x, mod, seq_of_tile, norm_g, w_main, w_a, dk, dv, tm):
    t, d = x.shape
    assert dk == TN and dv % TN == 0
    n_f32 = (2 * dk + dv) // TN
    hk = dk // GLA_HEADS
    return pl.pallas_call(
        functools.partial(_gla_in_kernel, q_scale=float(hk) ** -0.5, n_f32=n_f32),
        grid_spec=_grid_spec(
            1, (t // tm,),
            [pl.BlockSpec((tm, d), lambda i, s: (i, 0)),
             pl.BlockSpec((None, N_MOD, d), lambda i, s: (s[i], 0, 0)),
             pl.BlockSpec((1, d), lambda i, s: (0, 0)),
             _resident(w_main.shape),
             _resident(w_a.shape)],
            [pl.BlockSpec((tm, 2 * dk + dv), lambda i, s: (i, 0)),
             pl.BlockSpec((tm, dv), lambda i, s: (i, 0)),
             pl.BlockSpec((tm, LANES), lambda i, s: (i, 0))]),
        out_shape=[jax.ShapeDtypeStruct((t, 2 * dk + dv), F32), jax.ShapeDtypeStruct((t, dv), BF16),
                   jax.ShapeDtypeStruct((t, LANES), F32)],
        compiler_params=_params(("parallel",), 52),
        name="gla_in_proj",
    )(seq_of_tile, x, mod, norm_g.reshape(1, d), w_main, w_a)


def _log_sigmoid(x):
    return jnp.minimum(x, 0.0) - jnp.log1p(jnp.exp(-jnp.abs(x)))


def _gla_sweep_kernel(order_ref, first_ref, q_ref, k_ref, v_ref, a_ref, wa2_ref, ba_ref, *rest, reverse):
    if reverse:
        prev_ref, o_ref, st_ref = rest
    else:
        o_ref, st_ref = rest
    i = pl.program_id(0)
    L = GLA_CHUNK
    tc, dk = q_ref.shape
    dv = v_ref.shape[1]
    hk, hv = dk // GLA_HEADS, dv // GLA_HEADS

    @pl.when(first_ref[i] == 1)
    def _():
        st_ref[...] = jnp.zeros_like(st_ref)

    gate_pre = jnp.dot(a_ref[...].astype(BF16), wa2_ref[...], preferred_element_type=F32) + ba_ref[...]
    g_all = _log_sigmoid(gate_pre) * np.float32(1.0 / GLA_TAU)

    row = lax.broadcasted_iota(jnp.int32, (L, L), 0)
    col = lax.broadcasted_iota(jnp.int32, (L, L), 1)
    if reverse:
        tri = (col >= row).astype(BF16)
        keep = col > row
        i_ref, i_last = L - 1 - L // 2, 0
    else:
        tri = (col <= row).astype(BF16)
        keep = col <= row
        i_ref, i_last = L // 2, L - 1

    nchunks = tc // L
    for c in (range(nchunks - 1, -1, -1) if reverse else range(nchunks)):
        rs = slice(c * L, (c + 1) * L)
        gc = g_all[rs, :]
        g1 = gc.astype(BF16)
        r1 = gc - g1.astype(F32)
        g2 = r1.astype(BF16)
        g3 = (r1 - g2.astype(F32)).astype(BF16)
        b = (jnp.dot(tri, g1, preferred_element_type=F32) + jnp.dot(tri, g2, preferred_element_type=F32)
             + jnp.dot(tri, g3, preferred_element_type=F32))
        b_mid = b[i_ref:i_ref + 1, :]
        b_last = b[i_last:i_last + 1, :]
        qc = q_ref[rs, :]
        kc = k_ref[rs, :]
        qa = (qc * jnp.exp(b - b_mid)).astype(BF16)
        ka = (kc * jnp.exp(b_mid - b)).astype(BF16)
        qx = (qc * jnp.exp(b)).astype(BF16)
        kx = (kc * jnp.exp(b_last - b)).astype(BF16)
        decay = jnp.exp(b_last)
        vc = v_ref[rs, :]
        for h in range(GLA_HEADS):
            ks = slice(h * hk, (h + 1) * hk)
            vs = slice(h * hv, (h + 1) * hv)
            att = lax.dot_general(qa[:, ks], ka[:, ks], (((1,), (1,)), ((), ())), preferred_element_type=F32)
            att = jnp.where(keep, att, 0.0).astype(BF16)
            st = st_ref[h]
            o = (jnp.dot(att, vc[:, vs], preferred_element_type=F32)
                 + lax.dot_general(qx[:, ks], st.astype(BF16), (((1,), (1,)), ((), ())),
                                   preferred_element_type=F32))
            st_ref[h] = st * decay[:, ks] + lax.dot_general(vc[:, vs], kx[:, ks], (((0,), (0,)), ((), ())),
                                                            preferred_element_type=F32)
            if reverse:
                o = o + prev_ref[rs, vs]
            o_ref[rs, vs] = o


def _gla_sweep(zf, zv, a, wa2, ba, seq_lens, reverse, prev=None):
    t, dv = zv.shape
    dk = wa2.shape[1]
    tc = GLA_BLOCK
    nblk = t // tc
    first = np.zeros(nblk, np.int32)
    base = 0
    for n in seq_lens:
        assert n % tc == 0
        first[(base + n) // tc - 1 if reverse else base // tc] = 1
        base += n
    order = np.arange(nblk, dtype=np.int32)
    if reverse:
        order = order[::-1].copy()
        first = first[order]
    tok = lambda col: (lambda i, order, first: (order[i], col))
    in_specs = [pl.BlockSpec((tc, dk), tok(0)),
                pl.BlockSpec((tc, dk), tok(1)),
                pl.BlockSpec((tc, dv), tok(0)),
                pl.BlockSpec((tc, LANES), tok(0)),
                _resident(wa2.shape),
                _resident(ba.shape)]
    args = [zf, zf, zv, a, wa2, ba]
    if reverse:
        in_specs.append(pl.BlockSpec((tc, dv), tok(0)))
        args.append(prev)
    return pl.pallas_call(
        functools.partial(_gla_sweep_kernel, reverse=reverse),
        grid_spec=_grid_spec(
            2, (nblk,), in_specs, pl.BlockSpec((tc, dv), tok(0)),
            [pltpu.VMEM((GLA_HEADS, dv // GLA_HEADS, dk // GLA_HEADS), F32)]),
        out_shape=jax.ShapeDtypeStruct((t, dv), F32),
        compiler_params=_params(("arbitrary",), 48),
        name="gla_sweep_bwd" if reverse else "gla_sweep_fwd",
    )(jnp.asarray(order), jnp.asarray(first), *args)


def _gla_out_kernel(seq_ref, o_ref, r_ref, ng_ref, wo_ref, x_ref, mod_ref, out_ref):
    dv = o_ref.shape[1]
    hv = dv // GLA_HEADS
    o = o_ref[...]
    parts = []
    for h in range(GLA_HEADS):
        oh = o[:, h * hv:(h + 1) * hv]
        oh = oh * lax.rsqrt(jnp.mean(oh * oh, axis=-1, keepdims=True) + NORM_EPS)
        parts.append(oh * ng_ref[...])
    y = (jnp.concatenate(parts, axis=1) * _silu(r_ref[...])).astype(BF16)
    out_ref[...] = x_ref[...] + mod_ref[5:6, :] * jnp.dot(y, wo_ref[...], preferred_element_type=F32)


def _gla_out(x, mod, seq_of_tile, o, zf, norm_g, w_out, tm):
    t, d = x.shape
    dv = o.shape[1]
    r_col = (zf.shape[1] - dv) // dv
    assert r_col * dv + dv == zf.shape[1]
    return pl.pallas_call(
        _gla_out_kernel,
        grid_spec=_grid_spec(
            1, (t // tm,),
            [pl.BlockSpec((tm, dv), lambda i, s: (i, 0)),
             pl.BlockSpec((tm, dv), lambda i, s: (i, r_col)),
             pl.BlockSpec((1, dv // GLA_HEADS), lambda i, s: (0, 0)),
             _resident(w_out.shape),
             pl.BlockSpec((tm, d), lambda i, s: (i, 0)),
             pl.BlockSpec((None, N_MOD, d), lambda i, s: (s[i], 0, 0))],
            pl.BlockSpec((tm, d), lambda i, s: (i, 0))),
        out_shape=jax.ShapeDtypeStruct((t, d), F32),
        compiler_params=_params(("parallel",), 48),
        name="gla_out_proj",
    )(seq_of_tile, o, zf, norm_g.reshape(1, -1), w_out, x, mod)


def _final_norm_kernel(x_ref, g_ref, o_ref):
    x = x_ref[...]
    o_ref[...] = x * lax.rsqrt(jnp.mean(x * x, axis=-1, keepdims=True) + NORM_EPS) * g_ref[...]


def _final_norm(x, g):
    t, d = x.shape
    return pl.pallas_call(
        _final_norm_kernel,
        grid=(t // TM,),
        in_specs=[pl.BlockSpec((TM, d), lambda i: (i, 0)), pl.BlockSpec((1, d), lambda i: (0, 0))],
        out_specs=pl.BlockSpec((TM, d), lambda i: (i, 0)),
        out_shape=jax.ShapeDtypeStruct((t, d), F32),
        compiler_params=_params(("parallel",), 32),
        name="final_norm",
    )(x, g.reshape(1, d))


def _seq_table(seq_lens, tile):
    out = []
    for s, n in enumerate(seq_lens):
        assert n % tile == 0
        out += [s] * (n // tile)
    return jnp.asarray(np.array(out, np.int32))


def kernel(x_prompt, x_sample, c_prompt, c_sample, mod_w, mod_b, norm_g, ffn_w_gate, ffn_w_up, ffn_w_down,
           ab_w_in, gmlp_ln_g, gmlp_ln_b, gmlp_w_s, gmlp_b_s, na_rpb, ab_w_out,
           gla_w_in, gla_w_a2, gla_b_a, gla_norm_g, gla_w_out, final_norm_g):
    d = x_prompt.shape[-1]
    depth = mod_w.shape[0]
    seq_lens = (x_prompt.shape[1],) * x_prompt.shape[0] + (x_sample.shape[1],) * x_sample.shape[0]
    n_prompt = x_prompt.shape[0] * x_prompt.shape[1]

    x = jnp.concatenate([x_prompt.reshape(-1, d), x_sample.reshape(-1, d)], axis=0)
    c = jnp.concatenate([c_prompt, c_sample], axis=0)
    mod = _modulation(c, mod_w, mod_b)
    seq_tm = _seq_table(seq_lens, TM)
    seq_ffn = _seq_table(seq_lens, TM_FFN)
    gla_tm = TM // 2
    seq_gla = _seq_table(seq_lens, gla_tm)

    dk = gla_w_a2.shape[-1]
    dv = gla_w_out.shape[1]

    for i in range(depth):
        m = mod[i]
        x = _ffn(x, m, seq_ffn, norm_g[i, 0], ffn_w_gate[i, 0].astype(BF16), ffn_w_up[i, 0].astype(BF16),
                 ffn_w_down[i, 0].astype(BF16), 0)
        j = i // 2
        if i % 2 == 0:
            u, z = _ab_in(x, m, seq_tm, norm_g[i, 1], ab_w_in[j].astype(BF16), gmlp_ln_g[j], gmlp_ln_b[j])
            nb = _na(z, _na_bias_table(na_rpb[j]), seq_lens)
            x = _ab_out(x, m, seq_tm, u, z, nb, gmlp_w_s[j].astype(BF16), gmlp_b_s[j], ab_w_out[j].astype(BF16))
        else:
            w = gla_w_in[j]
            w_main = jnp.concatenate([w[:, :2 * dk], w[:, 2 * dk + dv:2 * dk + 2 * dv], w[:, 2 * dk:2 * dk + dv]],
                                     axis=1).astype(BF16)
            w_a = jnp.pad(w[:, 2 * dk + 2 * dv:], ((0, 0), (0, LANES - 2 * GLA_RANK))).astype(BF16)
            zf, zv, a = _gla_in(x, m, seq_gla, norm_g[i, 1], w_main, w_a, dk, dv, gla_tm)
            outs = None
            for z_dir in range(2):
                wa2 = jnp.zeros((LANES, dk), F32).at[z_dir * GLA_RANK:(z_dir + 1) * GLA_RANK].set(gla_w_a2[j, z_dir])
                outs = _gla_sweep(zf, zv, a, wa2.astype(BF16), gla_b_a[j, z_dir].reshape(1, dk).astype(F32),
                                  seq_lens, reverse=bool(z_dir), prev=outs)
            x = _gla_out(x, m, seq_gla, outs, zf, gla_norm_g[j], gla_w_out[j].astype(BF16), gla_tm)
        x = _ffn(x, m, seq_ffn, norm_g[i, 2], ffn_w_gate[i, 1].astype(BF16), ffn_w_up[i, 1].astype(BF16),
                 ffn_w_down[i, 1].astype(BF16), 6)
    y = _final_norm(x, final_norm_g)
    return (y[:n_prompt].reshape(x_prompt.shape), y[n_prompt:].reshape(x_sample.shape))
```

```python
import functools

import numpy as np
import jax
import jax.numpy as jnp
from jax import lax
from jax.experimental import pallas as pl
from jax.experimental.pallas import tpu as pltpu

F32 = jnp.float32
BF16 = jnp.bfloat16

NORM_EPS = 1e-6
N_MOD = 9
GRID_W = 64
GMLP_GROUPS = 8
GMLP_CHUNK = 128
NA_HEADS = 8
NA_KH = 8
NA_KW = 16
GLA_HEADS = 4
GLA_RANK = 16
GLA_TAU = 16.0
GLA_CHUNK = 64
MASK_VALUE = -1e30

LANES = 128
MIB = 2 ** 20

TM = 512
FFN_TILE = (1024, 256)
FFN_TILE_SPLIT = (512, 512)
NORM_ROWS = 32
TN = 1024
NA_ROWS = 8
NA_WIN_ROWS = 16
GLA_BLOCK = 512


def _params(dims, vmem_mib):
    return pltpu.CompilerParams(dimension_semantics=dims, vmem_limit_bytes=vmem_mib * MIB)


def _grid_spec(n_prefetch, grid, in_specs, out_specs, scratch=()):
    return pltpu.PrefetchScalarGridSpec(num_scalar_prefetch=n_prefetch, grid=grid, in_specs=in_specs,
                                        out_specs=out_specs, scratch_shapes=list(scratch))


def _resident(shape):
    nd = len(shape)
    return pl.BlockSpec(shape, lambda *_: (0,) * nd, pipeline_mode=pl.Buffered(1))


def _norm_mod(x, g, shift, scale):
    y = x * lax.rsqrt(jnp.mean(x * x, axis=-1, keepdims=True) + NORM_EPS) * g
    return y * (1.0 + scale) + shift


def _silu(x):
    return x * jax.nn.sigmoid(x)


def _gelu(x):
    return 0.5 * x * (1.0 + lax.erf(x * np.float32(np.sqrt(0.5))))


def _mod_kernel(c_ref, w_ref, b_ref, o_ref):
    c = c_ref[...]
    o_ref[...] = jnp.dot(_silu(c).astype(BF16), w_ref[...].astype(BF16),
                         preferred_element_type=F32) + b_ref[...]


def _modulation(c, mod_w, mod_b):
    depth, d, n = mod_w.shape
    nseq = c.shape[0]
    rows = -(-nseq // 16) * 16
    c_pad = jnp.pad(c, ((0, rows - nseq), (0, 0)))
    tn = 1024
    out = pl.pallas_call(
        _mod_kernel,
        grid=(depth, n // tn),
        in_specs=[pl.BlockSpec((rows, d), lambda l, j: (0, 0)),
                  pl.BlockSpec((None, d, tn), lambda l, j: (l, 0, j)),
                  pl.BlockSpec((None, 1, tn), lambda l, j: (l, 0, j))],
        out_specs=pl.BlockSpec((None, rows, tn), lambda l, j: (l, 0, j)),
        out_shape=jax.ShapeDtypeStruct((depth, rows, n), F32),
        compiler_params=_params(("arbitrary", "arbitrary"), 40),
        name="modulation",
    )(c_pad, mod_w, mod_b.reshape(depth, 1, n))
    return out[:, :nseq].reshape(depth, nseq, N_MOD, d)


def _ffn_kernel(seq_ref, *refs, mod_base, tiles_first):
    x_refs, (mod_ref, g_ref, wg_ref, wu_ref, wd_ref, o_ref, h_ref, inv_ref) = refs[:-8], refs[-8:]
    i = pl.program_id(0)
    j = pl.program_id(1)

    def start_tile(x_ref):
        n_chunks = x_ref.shape[0] // NORM_ROWS
        reps = x_ref.shape[1] // LANES

        def chunk(r):
            return pl.ds(pl.multiple_of(r * NORM_ROWS, NORM_ROWS), NORM_ROWS)

        def rms_body(r, carry):
            x = x_ref[chunk(r), :]
            inv = lax.rsqrt(jnp.mean(x * x, axis=-1, keepdims=True) + NORM_EPS)
            inv_ref[chunk(r), :] = jnp.broadcast_to(inv, (NORM_ROWS, LANES))
            return carry

        lax.fori_loop(0, n_chunks, rms_body, 0, unroll=4)
        gain = g_ref[...] * (1.0 + mod_ref[mod_base + 1:mod_base + 2, :])
        shift = mod_ref[mod_base:mod_base + 1, :]

        half = NORM_ROWS // 2

        def mod_body(r, carry):
            rows = pl.ds(pl.multiple_of(r * half, half), half)
            x = x_ref[rows, :]
            inv = jnp.concatenate([inv_ref[rows, :]] * reps, axis=1)
            h_ref[rows, :] = ((x * inv) * gain + shift).astype(BF16)
            o_ref[rows, :] = x
            return carry

        lax.fori_loop(0, 2 * n_chunks, mod_body, 0, unroll=4)

    if len(x_refs) == 1:
        pl.when(j == 0)(lambda: start_tile(x_refs[0]))
    else:
        pl.when(jnp.logical_and(j == 0, i < tiles_first))(lambda: start_tile(x_refs[0]))
        pl.when(jnp.logical_and(j == 0, i >= tiles_first))(lambda: start_tile(x_refs[1]))

    h = h_ref[...]
    gate = jnp.dot(h, wg_ref[...], preferred_element_type=F32)
    up = jnp.dot(h, wu_ref[...], preferred_element_type=F32)
    a = (_silu(gate) * up).astype(BF16)
    o_ref[...] += (0.5 * mod_ref[mod_base + 2:mod_base + 3, :]) * jnp.dot(a, wd_ref[...],
                                                                           preferred_element_type=F32)


def _ffn_vmem_mib(tm, tf, d, n_x):
    tiles = (n_x + 1) * 2 * tm * d * 4 + tm * d * 2
    weights = 2 * 3 * d * tf * 2
    temps = 3 * tm * tf * 4
    return (tiles + weights + temps) // MIB + 8


def _ffn(xs, mod, seq_lens, norm_g, wg, wu, wd, mod_base):
    d = xs[0].shape[1]
    t = sum(x.shape[0] for x in xs)
    f = wg.shape[1]
    tm, tf = FFN_TILE_SPLIT if len(xs) > 1 else FFN_TILE
    tok = lambda i, j, s: (i, 0)
    if len(xs) == 1:
        tiles_first = None
        x_specs = [pl.BlockSpec((tm, d), tok)]
    else:
        assert xs[0].shape[0] % tm == 0
        tiles_first = xs[0].shape[0] // tm
        x_specs = [pl.BlockSpec((tm, d), lambda i, j, s: (jnp.minimum(i, tiles_first - 1), 0)),
                   pl.BlockSpec((tm, d), lambda i, j, s: (jnp.maximum(i - tiles_first, 0), 0))]
    return pl.pallas_call(
        functools.partial(_ffn_kernel, mod_base=mod_base, tiles_first=tiles_first),
        grid_spec=_grid_spec(
            1, (t // tm, f // tf),
            x_specs + [
             pl.BlockSpec((None, N_MOD, d), lambda i, j, s: (s[i], 0, 0)),
             pl.BlockSpec((1, d), lambda i, j, s: (0, 0)),
             pl.BlockSpec((d, tf), lambda i, j, s: (0, j)),
             pl.BlockSpec((d, tf), lambda i, j, s: (0, j)),
             pl.BlockSpec((tf, d), lambda i, j, s: (j, 0))],
            pl.BlockSpec((tm, d), tok),
            [pltpu.VMEM((tm, d), BF16), pltpu.VMEM((tm, LANES), F32)]),
        out_shape=jax.ShapeDtypeStruct((t, d), F32),
        compiler_params=_params(("arbitrary", "arbitrary"), _ffn_vmem_mib(tm, tf, d, len(xs))),
        name="swiglu",
    )(_seq_table(seq_lens, tm), *xs, mod, norm_g.reshape(1, d), wg, wu, wd)


def _ab_in_kernel(seq_ref, x_ref, mod_ref, g_ref, w_ref, lng_ref, lnb_ref, u_ref, z_ref, *, q_scale):
    h = _norm_mod(x_ref[...], g_ref[...], mod_ref[3:4, :], mod_ref[4:5, :]).astype(BF16)

    def col(j):
        return jnp.dot(h, w_ref[:, j * TN:(j + 1) * TN], preferred_element_type=F32)

    u_ref[...] = _gelu(col(0))
    v = _gelu(col(1))
    vc = v - jnp.mean(v, axis=-1, keepdims=True)
    y = vc * lax.rsqrt(jnp.mean(vc * vc, axis=-1, keepdims=True) + NORM_EPS)
    z_ref[:, 0:TN] = (y * lng_ref[...] + lnb_ref[...]).astype(BF16)
    z_ref[:, TN:2 * TN] = (col(2) * q_scale).astype(BF16)
    z_ref[:, 2 * TN:3 * TN] = col(3).astype(BF16)
    z_ref[:, 3 * TN:4 * TN] = col(4).astype(BF16)


def _ab_in(x, mod, seq_of_tile, norm_g, w_in, ln_g, ln_b):
    t, d = x.shape
    n = w_in.shape[1]
    d_a = ln_g.shape[0]
    assert d_a == TN and n == 5 * TN
    head_dim = d_a // NA_HEADS
    return pl.pallas_call(
        functools.partial(_ab_in_kernel, q_scale=float(head_dim) ** -0.5),
        grid_spec=_grid_spec(
            1, (t // TM,),
            [pl.BlockSpec((TM, d), lambda i, s: (i, 0)),
             pl.BlockSpec((None, N_MOD, d), lambda i, s: (s[i], 0, 0)),
             pl.BlockSpec((1, d), lambda i, s: (0, 0)),
             _resident(w_in.shape),
             pl.BlockSpec((1, d_a), lambda i, s: (0, 0)),
             pl.BlockSpec((1, d_a), lambda i, s: (0, 0))],
            [pl.BlockSpec((TM, TN), lambda i, s: (i, 0)),
             pl.BlockSpec((TM, n - TN), lambda i, s: (i, 0))]),
        out_shape=[jax.ShapeDtypeStruct((t, TN), F32), jax.ShapeDtypeStruct((t, n - TN), BF16)],
        compiler_params=_params(("parallel",), 52),
        name="ab_in_proj",
    )(seq_of_tile, x, mod, norm_g.reshape(1, d), w_in, ln_g.reshape(1, d_a), ln_b.reshape(1, d_a))


def _na_kernel(lb_ref, rows_ref, win_ref, q_ref, k_ref, v_ref, bias_ref, o_ref):
    i = pl.program_id(0)
    lb = lb_ref[i]
    rows = rows_ref[i]
    head_dim = q_ref.shape[1] // NA_HEADS
    band = NA_KH * GRID_W
    win_row0 = jnp.clip(lb * NA_ROWS - NA_KH // 2, 0, rows - NA_WIN_ROWS)

    def row_body(rl, carry):
        r = lb * NA_ROWS + rl
        row_start = jnp.clip(r - NA_KH // 2, 0, rows - NA_KH)
        off = pl.multiple_of((row_start - win_row0) * GRID_W, GRID_W)
        shift = row_start - r + (NA_KH - 1)
        q_rows = pl.ds(pl.multiple_of(rl * GRID_W, GRID_W), GRID_W)
        heads = [slice(h * head_dim, (h + 1) * head_dim) for h in range(NA_HEADS)]
        scores = [lax.dot_general(q_ref[q_rows, cols], k_ref[pl.ds(off, band), cols], (((1,), (1,)), ((), ())),
                                  preferred_element_type=F32) for cols in heads]
        probs, norms = [], []
        for h in range(NA_HEADS):
            s = scores[h] + bias_ref[h, pl.ds(shift, 1)][0]
            p = jnp.exp(s - jnp.max(s, axis=-1, keepdims=True))
            norms.append(jnp.sum(p, axis=-1, keepdims=True))
            probs.append(p.astype(BF16))
        for h, cols in enumerate(heads):
            o = jnp.dot(probs[h], v_ref[pl.ds(off, band), cols], preferred_element_type=F32) / norms[h]
            o_ref[q_rows, cols] = o.astype(BF16)
        return carry

    lax.fori_loop(0, NA_ROWS, row_body, 0)


def _na_bias_table(rpb):
    qc = np.arange(GRID_W)
    kc = np.arange(GRID_W)
    cstart = np.clip(qc - NA_KW // 2, 0, GRID_W - NA_KW)
    valid = (kc[None, :] >= cstart[:, None]) & (kc[None, :] < cstart[:, None] + NA_KW)
    col_idx = np.clip(kc[None, :] - qc[:, None] + NA_KW - 1, 0, 2 * NA_KW - 2)
    onehot = (col_idx[None] == np.arange(2 * NA_KW - 1)[:, None, None]).astype(np.float32)
    cols = jnp.einsum('hij,jqk->hiqk', rpb.astype(F32), onehot, precision=lax.Precision.HIGHEST)
    cols = jnp.where(valid[None, None], cols, F32(MASK_VALUE))
    t = jnp.stack([cols[:, s:s + NA_KH] for s in range(NA_KH)], axis=1)
    t = jnp.transpose(t, (0, 1, 3, 2, 4))
    return t.reshape(rpb.shape[0], NA_KH, GRID_W, NA_KH * GRID_W)


def _na(z, bias, seq_lens):
    t = z.shape[0]
    d_b = z.shape[1] // 4
    blk = NA_ROWS * GRID_W
    win = NA_WIN_ROWS * GRID_W
    lb, rows, wstart = [], [], []
    base = 0
    for n in seq_lens:
        assert n % blk == 0 and n >= win
        for b in range(n // blk):
            lb.append(b)
            rows.append(n // GRID_W)
            wstart.append((base + int(np.clip(b * blk - (NA_KH // 2) * GRID_W, 0, n - win))) // GRID_W)
        base += n
    tables = [jnp.asarray(np.array(a, np.int32)) for a in (lb, rows, wstart)]
    return pl.pallas_call(
        _na_kernel,
        grid_spec=_grid_spec(
            3, (t // blk,),
            [pl.BlockSpec((blk, d_b), lambda i, lb, rows, ws: (i, 1)),
             pl.BlockSpec((pl.Element(win), pl.Element(d_b)), lambda i, lb, rows, ws: (ws[i] * GRID_W, 2 * d_b)),
             pl.BlockSpec((pl.Element(win), pl.Element(d_b)), lambda i, lb, rows, ws: (ws[i] * GRID_W, 3 * d_b)),
             _resident(bias.shape)],
            pl.BlockSpec((blk, d_b), lambda i, lb, rows, ws: (i, 0))),
        out_shape=jax.ShapeDtypeStruct((t, d_b), BF16),
        compiler_params=_params(("parallel",), 32),
        name="neighborhood_attention",
    )(*tables, z, z, z, bias)


def _ab_out_kernel(seq_ref, u_ref, z_ref, nb_ref, ws_ref, bs_ref, wo_ref, x_ref, mod_ref, o_ref):
    tm, d_a = u_ref.shape
    nch = tm // GMLP_CHUNK
    cg = d_a // GMLP_GROUPS
    vs = z_ref[...]
    s_blocks = [[None] * GMLP_GROUPS for _ in range(nch)]
    for g in range(GMLP_GROUPS):
        cat = jnp.concatenate([vs[n * GMLP_CHUNK:(n + 1) * GMLP_CHUNK, g * cg:(g + 1) * cg] for n in range(nch)],
                              axis=1)
        sg = jnp.dot(ws_ref[g], cat, preferred_element_type=F32)
        for n in range(nch):
            s_blocks[n][g] = sg[:, n * cg:(n + 1) * cg] + bs_ref[g]
    s = jnp.concatenate([jnp.concatenate(row, axis=1) for row in s_blocks], axis=0)
    a = (u_ref[...] * s).astype(BF16)
    y = (jnp.dot(a, wo_ref[0:d_a, :], preferred_element_type=F32)
         + jnp.dot(nb_ref[...], wo_ref[d_a:, :], preferred_element_type=F32))
    o_ref[...] = x_ref[...] + mod_ref[5:6, :] * y


def _ab_out(x, mod, seq_of_tile, u, z, nb, w_s, b_s, w_out):
    t, d = x.shape
    d_a = u.shape[1]
    cg = d_a // GMLP_GROUPS
    assert cg == LANES and GMLP_CHUNK == LANES
    bs_tile = jnp.broadcast_to(b_s.astype(F32)[:, :, None], (GMLP_GROUPS, GMLP_CHUNK, cg))
    return pl.pallas_call(
        _ab_out_kernel,
        grid_spec=_grid_spec(
            1, (t // TM,),
            [pl.BlockSpec((TM, d_a), lambda i, s: (i, 0)),
             pl.BlockSpec((TM, d_a), lambda i, s: (i, 0)),
             pl.BlockSpec((TM, nb.shape[1]), lambda i, s: (i, 0)),
             _resident(w_s.shape),
             _resident(bs_tile.shape),
             _resident(w_out.shape),
             pl.BlockSpec((TM, d), lambda i, s: (i, 0)),
             pl.BlockSpec((None, N_MOD, d), lambda i, s: (s[i], 0, 0))],
            pl.BlockSpec((TM, d), lambda i, s: (i, 0))),
        out_shape=jax.ShapeDtypeStruct((t, d), F32),
        compiler_params=_params(("parallel",), 48),
        name="ab_out_proj",
    )(seq_of_tile, u, z, nb, w_s, bs_tile, w_out, x, mod)


def _gla_in_kernel(seq_ref, x_ref, mod_ref, g_ref, w_ref, wa_ref, zf_ref, zv_ref, a_ref, *, q_scale, n_f32):
    h = _norm_mod(x_ref[...], g_ref[...], mod_ref[3:4, :], mod_ref[4:5, :]).astype(BF16)
    a_ref[...] = jnp.dot(h, wa_ref[...], preferred_element_type=F32)
    for j in range(w_ref.shape[1] // TN):
        z = jnp.dot(h, w_ref[:, j * TN:(j + 1) * TN], preferred_element_type=F32)
        if j == 0:
            z = z * q_scale
        if j < n_f32:
            zf_ref[:, j * TN:(j + 1) * TN] = z
        else:
            zv_ref[:, (j - n_f32) * TN:(j - n_f32 + 1) * TN] = z.astype(BF16)


def _gla_in(x, mod, seq_of_tile, norm_g, w_main, w_a, dk, dv, tm):
    t, d = x.shape
    assert dk == TN and dv % TN == 0
    n_f32 = (2 * dk + dv) // TN
    hk = dk // GLA_HEADS
    return pl.pallas_call(
        functools.partial(_gla_in_kernel, q_scale=float(hk) ** -0.5, n_f32=n_f32),
        grid_spec=_grid_spec(
            1, (t // tm,),
            [pl.BlockSpec((tm, d), lambda i, s: (i, 0)),
             pl.BlockSpec((None, N_MOD, d), lambda i, s: (s[i], 0, 0)),
             pl.BlockSpec((1, d), lambda i, s: (0, 0)),
             _resident(w_main.shape),
             _resident(w_a.shape)],
            [pl.BlockSpec((tm, 2 * dk + dv), lambda i, s: (i, 0)),
             pl.BlockSpec((tm, dv), lambda i, s: (i, 0)),
             pl.BlockSpec((tm, LANES), lambda i, s: (i, 0))]),
        out_shape=[jax.ShapeDtypeStruct((t, 2 * dk + dv), F32), jax.ShapeDtypeStruct((t, dv), BF16),
                   jax.ShapeDtypeStruct((t, LANES), F32)],
        compiler_params=_params(("parallel",), 52),
        name="gla_in_proj",
    )(seq_of_tile, x, mod, norm_g.reshape(1, d), w_main, w_a)


def _log_sigmoid(x):
    return jnp.minimum(x, 0.0) - jnp.log1p(jnp.exp(-jnp.abs(x)))


def _gla_sweep_kernel(order_ref, first_ref, q_ref, k_ref, v_ref, a_ref, wa2_ref, ba_ref, *rest, reverse):
    if reverse:
        prev_ref, o_ref, st_ref = rest
    else:
        o_ref, st_ref = rest
    i = pl.program_id(0)
    L = GLA_CHUNK
    tc, dk = q_ref.shape
    dv = v_ref.shape[1]
    hk, hv = dk // GLA_HEADS, dv // GLA_HEADS

    @pl.when(first_ref[i] == 1)
    def _():
        st_ref[...] = jnp.zeros_like(st_ref)

    gate_pre = jnp.dot(a_ref[...].astype(BF16), wa2_ref[...], preferred_element_type=F32) + ba_ref[...]
    g_all = _log_sigmoid(gate_pre) * np.float32(1.0 / GLA_TAU)

    row = lax.broadcasted_iota(jnp.int32, (L, L), 0)
    col = lax.broadcasted_iota(jnp.int32, (L, L), 1)
    if reverse:
        tri = (col >= row).astype(BF16)
        keep = col > row
        i_ref, i_last = L - 1 - L // 2, 0
    else:
        tri = (col <= row).astype(BF16)
        keep = col <= row
        i_ref, i_last = L // 2, L - 1

    heads = [(slice(h * hk, (h + 1) * hk), slice(h * hv, (h + 1) * hv)) for h in range(GLA_HEADS)]
    nt = (((1,), (1,)), ((), ()))
    tn = (((0,), (0,)), ((), ()))

    nchunks = tc // L
    for c in (range(nchunks - 1, -1, -1) if reverse else range(nchunks)):
        rs = slice(c * L, (c + 1) * L)
        gc = g_all[rs, :]
        g1 = gc.astype(BF16)
        r1 = gc - g1.astype(F32)
        g2 = r1.astype(BF16)
        g3 = (r1 - g2.astype(F32)).astype(BF16)
        b = (jnp.dot(tri, g1, preferred_element_type=F32) + jnp.dot(tri, g2, preferred_element_type=F32)
             + jnp.dot(tri, g3, preferred_element_type=F32))
        b_mid = b[i_ref:i_ref + 1, :]
        b_last = b[i_last:i_last + 1, :]
        qc = q_ref[rs, :]
        kc = k_ref[rs, :]
        qa = (qc * jnp.exp(b - b_mid)).astype(BF16)
        ka = (kc * jnp.exp(b_mid - b)).astype(BF16)
        qx = (qc * jnp.exp(b)).astype(BF16)
        kx = (kc * jnp.exp(b_last - b)).astype(BF16)
        decay = jnp.exp(b_last)
        vc = v_ref[rs, :]
        for h, (ks, vs) in enumerate(heads):
            att = lax.dot_general(qa[:, ks], ka[:, ks], nt, preferred_element_type=F32)
            att = jnp.where(keep, att, 0.0).astype(BF16)
            st = st_ref[h]
            o = (jnp.dot(att, vc[:, vs], preferred_element_type=F32)
                 + lax.dot_general(qx[:, ks], st.astype(BF16), nt, preferred_element_type=F32))
            st_ref[h] = st * decay[:, ks] + lax.dot_general(vc[:, vs], kx[:, ks], tn, preferred_element_type=F32)
            if reverse:
                o = o + prev_ref[rs, vs]
            o_ref[rs, vs] = o


def _gla_sweep(zf, zv, a, wa2, ba, seq_lens, reverse, prev=None):
    t, dv = zv.shape
    dk = wa2.shape[1]
    tc = GLA_BLOCK
    nblk = t // tc
    first = np.zeros(nblk, np.int32)
    base = 0
    for n in seq_lens:
        assert n % tc == 0
        first[(base + n) // tc - 1 if reverse else base // tc] = 1
        base += n
    order = np.arange(nblk, dtype=np.int32)
    if reverse:
        order = order[::-1].copy()
        first = first[order]
    tok = lambda col: (lambda i, order, first: (order[i], col))
    in_specs = [pl.BlockSpec((tc, dk), tok(0)),
                pl.BlockSpec((tc, dk), tok(1)),
                pl.BlockSpec((tc, dv), tok(0)),
                pl.BlockSpec((tc, LANES), tok(0)),
                _resident(wa2.shape),
                _resident(ba.shape)]
    args = [zf, zf, zv, a, wa2, ba]
    if reverse:
        in_specs.append(pl.BlockSpec((tc, dv), tok(0)))
        args.append(prev)
    return pl.pallas_call(
        functools.partial(_gla_sweep_kernel, reverse=reverse),
        grid_spec=_grid_spec(
            2, (nblk,), in_specs, pl.BlockSpec((tc, dv), tok(0)),
            [pltpu.VMEM((GLA_HEADS, dv // GLA_HEADS, dk // GLA_HEADS), F32)]),
        out_shape=jax.ShapeDtypeStruct((t, dv), F32),
        compiler_params=_params(("arbitrary",), 48),
        name="gla_sweep_bwd" if reverse else "gla_sweep_fwd",
    )(jnp.asarray(order), jnp.asarray(first), *args)


def _gla_out_kernel(seq_ref, o_ref, r_ref, ng_ref, wo_ref, x_ref, mod_ref, out_ref):
    dv = o_ref.shape[1]
    hv = dv // GLA_HEADS
    o = o_ref[...]
    parts = []
    for h in range(GLA_HEADS):
        oh = o[:, h * hv:(h + 1) * hv]
        oh = oh * lax.rsqrt(jnp.mean(oh * oh, axis=-1, keepdims=True) + NORM_EPS)
        parts.append(oh * ng_ref[...])
    y = (jnp.concatenate(parts, axis=1) * _silu(r_ref[...])).astype(BF16)
    out_ref[...] = x_ref[...] + mod_ref[5:6, :] * jnp.dot(y, wo_ref[...], preferred_element_type=F32)


def _gla_out(x, mod, seq_of_tile, o, zf, norm_g, w_out, tm):
    t, d = x.shape
    dv = o.shape[1]
    r_col = (zf.shape[1] - dv) // dv
    assert r_col * dv + dv == zf.shape[1]
    return pl.pallas_call(
        _gla_out_kernel,
        grid_spec=_grid_spec(
            1, (t // tm,),
            [pl.BlockSpec((tm, dv), lambda i, s: (i, 0)),
             pl.BlockSpec((tm, dv), lambda i, s: (i, r_col)),
             pl.BlockSpec((1, dv // GLA_HEADS), lambda i, s: (0, 0)),
             _resident(w_out.shape),
             pl.BlockSpec((tm, d), lambda i, s: (i, 0)),
             pl.BlockSpec((None, N_MOD, d), lambda i, s: (s[i], 0, 0))],
            pl.BlockSpec((tm, d), lambda i, s: (i, 0))),
        out_shape=jax.ShapeDtypeStruct((t, d), F32),
        compiler_params=_params(("parallel",), 48),
        name="gla_out_proj",
    )(seq_of_tile, o, zf, norm_g.reshape(1, -1), w_out, x, mod)


def _final_norm_kernel(x_ref, g_ref, op_ref, os_ref, *, tiles_p):
    i = pl.program_id(0)
    x = x_ref[...]
    y = x * lax.rsqrt(jnp.mean(x * x, axis=-1, keepdims=True) + NORM_EPS) * g_ref[...]

    @pl.when(i < tiles_p)
    def _():
        op_ref[...] = y

    @pl.when(i >= tiles_p)
    def _():
        os_ref[...] = y


def _final_norm(x, g, n_prompt):
    t, d = x.shape
    assert n_prompt % TM == 0
    tiles_p = n_prompt // TM
    return pl.pallas_call(
        functools.partial(_final_norm_kernel, tiles_p=tiles_p),
        grid=(t // TM,),
        in_specs=[pl.BlockSpec((TM, d), lambda i: (i, 0)), pl.BlockSpec((1, d), lambda i: (0, 0))],
        out_specs=[pl.BlockSpec((TM, d), lambda i: (jnp.minimum(i, tiles_p - 1), 0)),
                   pl.BlockSpec((TM, d), lambda i: (jnp.maximum(i - tiles_p, 0), 0))],
        out_shape=[jax.ShapeDtypeStruct((n_prompt, d), F32), jax.ShapeDtypeStruct((t - n_prompt, d), F32)],
        compiler_params=_params(("arbitrary",), 32),
        name="final_norm",
    )(x, g.reshape(1, d))


def _seq_table(seq_lens, tile):
    out = []
    for s, n in enumerate(seq_lens):
        assert n % tile == 0
        out += [s] * (n // tile)
    return jnp.asarray(np.array(out, np.int32))


def kernel(x_prompt, x_sample, c_prompt, c_sample, mod_w, mod_b, norm_g, ffn_w_gate, ffn_w_up, ffn_w_down,
           ab_w_in, gmlp_ln_g, gmlp_ln_b, gmlp_w_s, gmlp_b_s, na_rpb, ab_w_out,
           gla_w_in, gla_w_a2, gla_b_a, gla_norm_g, gla_w_out, final_norm_g):
    d = x_prompt.shape[-1]
    depth = mod_w.shape[0]
    seq_lens = (x_prompt.shape[1],) * x_prompt.shape[0] + (x_sample.shape[1],) * x_sample.shape[0]
    n_prompt = x_prompt.shape[0] * x_prompt.shape[1]

    xs = [x_prompt.reshape(-1, d), x_sample.reshape(-1, d)]
    c = jnp.concatenate([c_prompt, c_sample], axis=0)
    mod = _modulation(c, mod_w, mod_b)
    seq_tm = _seq_table(seq_lens, TM)
    gla_tm = TM // 2
    seq_gla = _seq_table(seq_lens, gla_tm)

    dk = gla_w_a2.shape[-1]
    dv = gla_w_out.shape[1]

    for i in range(depth):
        m = mod[i]
        x = _ffn(xs if i == 0 else [x], m, seq_lens,norm_g[i, 0], ffn_w_gate[i, 0].astype(BF16),
                 ffn_w_up[i, 0].astype(BF16), ffn_w_down[i, 0].astype(BF16), 0)
        j = i // 2
        if i % 2 == 0:
            u, z = _ab_in(x, m, seq_tm, norm_g[i, 1], ab_w_in[j].astype(BF16), gmlp_ln_g[j], gmlp_ln_b[j])
            nb = _na(z, _na_bias_table(na_rpb[j]), seq_lens)
            x = _ab_out(x, m, seq_tm, u, z, nb, gmlp_w_s[j].astype(BF16), gmlp_b_s[j], ab_w_out[j].astype(BF16))
        else:
            w = gla_w_in[j]
            w_main = jnp.concatenate([w[:, :2 * dk], w[:, 2 * dk + dv:2 * dk + 2 * dv], w[:, 2 * dk:2 * dk + dv]],
                                     axis=1).astype(BF16)
            w_a = jnp.pad(w[:, 2 * dk + 2 * dv:], ((0, 0), (0, LANES - 2 * GLA_RANK))).astype(BF16)
            zf, zv, a = _gla_in(x, m, seq_gla, norm_g[i, 1], w_main, w_a, dk, dv, gla_tm)
            outs = None
            for z_dir in range(2):
                wa2 = jnp.zeros((LANES, dk), F32).at[z_dir * GLA_RANK:(z_dir + 1) * GLA_RANK].set(gla_w_a2[j, z_dir])
                outs = _gla_sweep(zf, zv, a, wa2.astype(BF16), gla_b_a[j, z_dir].reshape(1, dk).astype(F32),
                                  seq_lens, reverse=bool(z_dir), prev=outs)
            x = _gla_out(x, m, seq_gla, outs, zf, gla_norm_g[j], gla_w_out[j].astype(BF16), gla_tm)
        x = _ffn([x], m, seq_lens,norm_g[i, 2], ffn_w_gate[i, 1].astype(BF16), ffn_w_up[i, 1].astype(BF16),
                 ffn_w_down[i, 1].astype(BF16), 6)
    y_prompt, y_sample = _final_norm(x, final_norm_g, n_prompt)
    return (y_prompt.reshape(x_prompt.shape), y_sample.reshape(x_sample.shape))
```

```python
import functools

import numpy as np
import jax
import jax.numpy as jnp
from jax import lax
from jax.experimental import pallas as pl
from jax.experimental.pallas import tpu as pltpu

F32 = jnp.float32
BF16 = jnp.bfloat16

NORM_EPS = 1e-6
N_MOD = 9
GRID_W = 64
GMLP_GROUPS = 8
GMLP_CHUNK = 128
NA_HEADS = 8
NA_KH = 8
NA_KW = 16
GLA_HEADS = 4
GLA_RANK = 16
GLA_TAU = 16.0
GLA_CHUNK = 64
MASK_VALUE = -1e30

LANES = 128
MIB = 2 ** 20

TM = 512
FFN_TILE = (1024, 512)
FFN_SUB = 256
FFN_TILE_SPLIT = (512, 512)
NORM_ROWS = 32
TN = 1024
NA_ROWS = 8
NA_WIN_ROWS = 16
GLA_BLOCK = 512


def _params(dims, vmem_mib):
    return pltpu.CompilerParams(dimension_semantics=dims, vmem_limit_bytes=vmem_mib * MIB)


def _grid_spec(n_prefetch, grid, in_specs, out_specs, scratch=()):
    return pltpu.PrefetchScalarGridSpec(num_scalar_prefetch=n_prefetch, grid=grid, in_specs=in_specs,
                                        out_specs=out_specs, scratch_shapes=list(scratch))


def _resident(shape, lead=()):
    block = (None,) * len(lead) + tuple(shape[len(lead):])
    index = tuple(lead) + (0,) * (len(shape) - len(lead))
    return pl.BlockSpec(block, lambda *_: index, pipeline_mode=pl.Buffered(1))


def _norm_mod(x, g, shift, scale):
    y = x * lax.rsqrt(jnp.mean(x * x, axis=-1, keepdims=True) + NORM_EPS) * g
    return y * (1.0 + scale) + shift


def _silu(x):
    return x * jax.nn.sigmoid(x)


def _gelu(x):
    return 0.5 * x * (1.0 + lax.erf(x * np.float32(np.sqrt(0.5))))


def _mod_kernel(c_ref, w_ref, b_ref, o_ref):
    c = c_ref[...]
    o_ref[...] = jnp.dot(_silu(c).astype(BF16), w_ref[...].astype(BF16),
                         preferred_element_type=F32) + b_ref[...]


def _modulation(c, mod_w, mod_b):
    depth, d, n = mod_w.shape
    nseq = c.shape[0]
    rows = -(-nseq // 16) * 16
    c_pad = jnp.pad(c, ((0, rows - nseq), (0, 0)))
    tn = 1024
    out = pl.pallas_call(
        _mod_kernel,
        grid=(depth, n // tn),
        in_specs=[pl.BlockSpec((rows, d), lambda l, j: (0, 0)),
                  pl.BlockSpec((None, d, tn), lambda l, j: (l, 0, j)),
                  pl.BlockSpec((None, 1, tn), lambda l, j: (l, 0, j))],
        out_specs=pl.BlockSpec((None, rows, tn), lambda l, j: (l, 0, j)),
        out_shape=jax.ShapeDtypeStruct((depth, rows, n), F32),
        compiler_params=_params(("arbitrary", "arbitrary"), 40),
        name="modulation",
    )(c_pad, mod_w, mod_b.reshape(depth, 1, n))
    return out[:, :nseq].reshape(depth, nseq, N_MOD, d)


def _ffn_kernel(seq_ref, *refs, mod_base, tiles_first):
    x_refs, (mod_ref, g_ref, wg_ref, wu_ref, wd_ref, o_ref, h_ref, inv_ref) = refs[:-8], refs[-8:]
    i = pl.program_id(0)
    j = pl.program_id(1)

    def start_tile(x_ref):
        n_chunks = x_ref.shape[0] // NORM_ROWS
        reps = x_ref.shape[1] // LANES

        def chunk(r):
            return pl.ds(pl.multiple_of(r * NORM_ROWS, NORM_ROWS), NORM_ROWS)

        def rms_body(r, carry):
            x = x_ref[chunk(r), :]
            inv = lax.rsqrt(jnp.mean(x * x, axis=-1, keepdims=True) + NORM_EPS)
            inv_ref[chunk(r), :] = jnp.broadcast_to(inv, (NORM_ROWS, LANES))
            return carry

        lax.fori_loop(0, n_chunks, rms_body, 0, unroll=4)
        gain = g_ref[...] * (1.0 + mod_ref[mod_base + 1:mod_base + 2, :])
        shift = mod_ref[mod_base:mod_base + 1, :]

        half = NORM_ROWS // 2

        def mod_body(r, carry):
            rows = pl.ds(pl.multiple_of(r * half, half), half)
            x = x_ref[rows, :]
            inv = jnp.concatenate([inv_ref[rows, :]] * reps, axis=1)
            h_ref[rows, :] = ((x * inv) * gain + shift).astype(BF16)
            o_ref[rows, :] = x
            return carry

        lax.fori_loop(0, 2 * n_chunks, mod_body, 0, unroll=4)

    if len(x_refs) == 1:
        pl.when(j == 0)(lambda: start_tile(x_refs[0]))
    else:
        pl.when(jnp.logical_and(j == 0, i < tiles_first))(lambda: start_tile(x_refs[0]))
        pl.when(jnp.logical_and(j == 0, i >= tiles_first))(lambda: start_tile(x_refs[1]))

    h = h_ref[...]
    res_gate = 0.5 * mod_ref[mod_base + 2:mod_base + 3, :]
    for c in range(wg_ref.shape[1] // FFN_SUB):
        cols = slice(c * FFN_SUB, (c + 1) * FFN_SUB)
        gate = jnp.dot(h, wg_ref[:, cols], preferred_element_type=F32)
        up = jnp.dot(h, wu_ref[:, cols], preferred_element_type=F32)
        a = (_silu(gate) * up).astype(BF16)
        o_ref[...] += res_gate * jnp.dot(a, wd_ref[cols, :], preferred_element_type=F32)


def _ffn_vmem_mib(tm, tf, d, n_x):
    tiles = (n_x + 1) * 2 * tm * d * 4 + tm * d * 2
    weights = 2 * 3 * d * tf * 2
    temps = 3 * tm * FFN_SUB * 4
    return (tiles + weights + temps) // MIB + 7


def _ffn(xs, mod, seq_lens, norm_g, wg, wu, wd, which, mod_base):
    d = xs[0].shape[1]
    t = sum(x.shape[0] for x in xs)
    f = wg.shape[-1]
    lyr, slot = which
    tm, tf = FFN_TILE_SPLIT if len(xs) > 1 else FFN_TILE
    tok = lambda i, j, s: (i, 0)
    if len(xs) == 1:
        tiles_first = None
        x_specs = [pl.BlockSpec((tm, d), tok)]
    else:
        assert xs[0].shape[0] % tm == 0
        tiles_first = xs[0].shape[0] // tm
        x_specs = [pl.BlockSpec((tm, d), lambda i, j, s: (jnp.minimum(i, tiles_first - 1), 0)),
                   pl.BlockSpec((tm, d), lambda i, j, s: (jnp.maximum(i - tiles_first, 0), 0))]
    return pl.pallas_call(
        functools.partial(_ffn_kernel, mod_base=mod_base, tiles_first=tiles_first),
        grid_spec=_grid_spec(
            1, (t // tm, f // tf),
            x_specs + [
             pl.BlockSpec((None, N_MOD, d), lambda i, j, s: (s[i], 0, 0)),
             pl.BlockSpec((1, d), lambda i, j, s: (0, 0)),
             pl.BlockSpec((None, None, d, tf), lambda i, j, s: (lyr, slot, 0, j)),
             pl.BlockSpec((None, None, d, tf), lambda i, j, s: (lyr, slot, 0, j)),
             pl.BlockSpec((None, None, tf, d), lambda i, j, s: (lyr, slot, j, 0))],
            pl.BlockSpec((tm, d), tok),
            [pltpu.VMEM((tm, d), BF16), pltpu.VMEM((tm, LANES), F32)]),
        out_shape=jax.ShapeDtypeStruct((t, d), F32),
        compiler_params=_params(("arbitrary", "arbitrary"), _ffn_vmem_mib(tm, tf, d, len(xs))),
        name="swiglu",
    )(_seq_table(seq_lens, tm), *xs, mod, norm_g.reshape(1, d), wg, wu, wd)


def _ab_in_kernel(seq_ref, x_ref, mod_ref, g_ref, w_ref, lng_ref, lnb_ref, u_ref, z_ref, *, q_scale):
    h = _norm_mod(x_ref[...], g_ref[...], mod_ref[3:4, :], mod_ref[4:5, :]).astype(BF16)

    def col(j):
        return jnp.dot(h, w_ref[:, j * TN:(j + 1) * TN], preferred_element_type=F32)

    u_ref[...] = _gelu(col(0))
    v = _gelu(col(1))
    vc = v - jnp.mean(v, axis=-1, keepdims=True)
    y = vc * lax.rsqrt(jnp.mean(vc * vc, axis=-1, keepdims=True) + NORM_EPS)
    z_ref[:, 0:TN] = (y * lng_ref[...] + lnb_ref[...]).astype(BF16)
    z_ref[:, TN:2 * TN] = (col(2) * q_scale).astype(BF16)
    z_ref[:, 2 * TN:3 * TN] = col(3).astype(BF16)
    z_ref[:, 3 * TN:4 * TN] = col(4).astype(BF16)


def _ab_in(x, mod, seq_of_tile, norm_g, w_in, lyr, ln_g, ln_b):
    t, d = x.shape
    n = w_in.shape[-1]
    d_a = ln_g.shape[0]
    assert d_a == TN and n == 5 * TN
    head_dim = d_a // NA_HEADS
    return pl.pallas_call(
        functools.partial(_ab_in_kernel, q_scale=float(head_dim) ** -0.5),
        grid_spec=_grid_spec(
            1, (t // TM,),
            [pl.BlockSpec((TM, d), lambda i, s: (i, 0)),
             pl.BlockSpec((None, N_MOD, d), lambda i, s: (s[i], 0, 0)),
             pl.BlockSpec((1, d), lambda i, s: (0, 0)),
             _resident(w_in.shape, (lyr,)),
             pl.BlockSpec((1, d_a), lambda i, s: (0, 0)),
             pl.BlockSpec((1, d_a), lambda i, s: (0, 0))],
            [pl.BlockSpec((TM, TN), lambda i, s: (i, 0)),
             pl.BlockSpec((TM, n - TN), lambda i, s: (i, 0))]),
        out_shape=[jax.ShapeDtypeStruct((t, TN), F32), jax.ShapeDtypeStruct((t, n - TN), BF16)],
        compiler_params=_params(("parallel",), 52),
        name="ab_in_proj",
    )(seq_of_tile, x, mod, norm_g.reshape(1, d), w_in, ln_g.reshape(1, d_a), ln_b.reshape(1, d_a))


def _na_kernel(lb_ref, rows_ref, win_ref, q_ref, k_ref, v_ref, bias_ref, o_ref):
    i = pl.program_id(0)
    lb = lb_ref[i]
    rows = rows_ref[i]
    head_dim = q_ref.shape[1] // NA_HEADS
    band = NA_KH * GRID_W
    win_row0 = jnp.clip(lb * NA_ROWS - NA_KH // 2, 0, rows - NA_WIN_ROWS)

    def row_body(rl, carry):
        r = lb * NA_ROWS + rl
        row_start = jnp.clip(r - NA_KH // 2, 0, rows - NA_KH)
        off = pl.multiple_of((row_start - win_row0) * GRID_W, GRID_W)
        shift = row_start - r + (NA_KH - 1)
        q_rows = pl.ds(pl.multiple_of(rl * GRID_W, GRID_W), GRID_W)
        heads = [slice(h * head_dim, (h + 1) * head_dim) for h in range(NA_HEADS)]
        scores = [lax.dot_general(q_ref[q_rows, cols], k_ref[pl.ds(off, band), cols], (((1,), (1,)), ((), ())),
                                  preferred_element_type=F32) for cols in heads]
        probs, norms = [], []
        for h in range(NA_HEADS):
            s = scores[h] + bias_ref[h, pl.ds(shift, 1)][0]
            p = jnp.exp(s - jnp.max(s, axis=-1, keepdims=True))
            norms.append(jnp.sum(p, axis=-1, keepdims=True))
            probs.append(p.astype(BF16))
        for h, cols in enumerate(heads):
            o = jnp.dot(probs[h], v_ref[pl.ds(off, band), cols], preferred_element_type=F32) / norms[h]
            o_ref[q_rows, cols] = o.astype(BF16)
        return carry

    lax.fori_loop(0, NA_ROWS, row_body, 0)


def _na_bias_table(rpb):
    qc = np.arange(GRID_W)
    kc = np.arange(GRID_W)
    cstart = np.clip(qc - NA_KW // 2, 0, GRID_W - NA_KW)
    valid = (kc[None, :] >= cstart[:, None]) & (kc[None, :] < cstart[:, None] + NA_KW)
    col_idx = np.clip(kc[None, :] - qc[:, None] + NA_KW - 1, 0, 2 * NA_KW - 2)
    onehot = (col_idx[None] == np.arange(2 * NA_KW - 1)[:, None, None]).astype(np.float32)
    cols = jnp.einsum('hij,jqk->hiqk', rpb.astype(F32), onehot, precision=lax.Precision.HIGHEST)
    cols = jnp.where(valid[None, None], cols, F32(MASK_VALUE))
    t = jnp.stack([cols[:, s:s + NA_KH] for s in range(NA_KH)], axis=1)
    t = jnp.transpose(t, (0, 1, 3, 2, 4))
    return t.reshape(rpb.shape[0], NA_KH, GRID_W, NA_KH * GRID_W)


def _na(z, bias, seq_lens):
    t = z.shape[0]
    d_b = z.shape[1] // 4
    blk = NA_ROWS * GRID_W
    win = NA_WIN_ROWS * GRID_W
    lb, rows, wstart = [], [], []
    base = 0
    for n in seq_lens:
        assert n % blk == 0 and n >= win
        for b in range(n // blk):
            lb.append(b)
            rows.append(n // GRID_W)
            wstart.append((base + int(np.clip(b * blk - (NA_KH // 2) * GRID_W, 0, n - win))) // GRID_W)
        base += n
    tables = [jnp.asarray(np.array(a, np.int32)) for a in (lb, rows, wstart)]
    return pl.pallas_call(
        _na_kernel,
        grid_spec=_grid_spec(
            3, (t // blk,),
            [pl.BlockSpec((blk, d_b), lambda i, lb, rows, ws: (i, 1)),
             pl.BlockSpec((pl.Element(win), pl.Element(d_b)), lambda i, lb, rows, ws: (ws[i] * GRID_W, 2 * d_b)),
             pl.BlockSpec((pl.Element(win), pl.Element(d_b)), lambda i, lb, rows, ws: (ws[i] * GRID_W, 3 * d_b)),
             _resident(bias.shape)],
            pl.BlockSpec((blk, d_b), lambda i, lb, rows, ws: (i, 0))),
        out_shape=jax.ShapeDtypeStruct((t, d_b), BF16),
        compiler_params=_params(("parallel",), 32),
        name="neighborhood_attention",
    )(*tables, z, z, z, bias)


def _ab_out_kernel(seq_ref, u_ref, z_ref, nb_ref, ws_ref, bs_ref, wo_ref, x_ref, mod_ref, o_ref):
    tm, d_a = u_ref.shape
    nch = tm // GMLP_CHUNK
    cg = d_a // GMLP_GROUPS
    vs = z_ref[...]
    s_blocks = [[None] * GMLP_GROUPS for _ in range(nch)]
    for g in range(GMLP_GROUPS):
        cat = jnp.concatenate([vs[n * GMLP_CHUNK:(n + 1) * GMLP_CHUNK, g * cg:(g + 1) * cg] for n in range(nch)],
                              axis=1)
        sg = jnp.dot(ws_ref[g], cat, preferred_element_type=F32)
        for n in range(nch):
            s_blocks[n][g] = sg[:, n * cg:(n + 1) * cg] + bs_ref[g]
    s = jnp.concatenate([jnp.concatenate(row, axis=1) for row in s_blocks], axis=0)
    a = (u_ref[...] * s).astype(BF16)
    y = (jnp.dot(a, wo_ref[0:d_a, :], preferred_element_type=F32)
         + jnp.dot(nb_ref[...], wo_ref[d_a:, :], preferred_element_type=F32))
    o_ref[...] = x_ref[...] + mod_ref[5:6, :] * y


def _ab_out(x, mod, seq_of_tile, u, z, nb, w_s, b_s, w_out, lyr):
    t, d = x.shape
    d_a = u.shape[1]
    cg = d_a // GMLP_GROUPS
    assert cg == LANES and GMLP_CHUNK == LANES
    bs_tile = jnp.broadcast_to(b_s.astype(F32)[:, :, None], (GMLP_GROUPS, GMLP_CHUNK, cg))
    return pl.pallas_call(
        _ab_out_kernel,
        grid_spec=_grid_spec(
            1, (t // TM,),
            [pl.BlockSpec((TM, d_a), lambda i, s: (i, 0)),
             pl.BlockSpec((TM, d_a), lambda i, s: (i, 0)),
             pl.BlockSpec((TM, nb.shape[1]), lambda i, s: (i, 0)),
             _resident(w_s.shape, (lyr,)),
             _resident(bs_tile.shape),
             _resident(w_out.shape, (lyr,)),
             pl.BlockSpec((TM, d), lambda i, s: (i, 0)),
             pl.BlockSpec((None, N_MOD, d), lambda i, s: (s[i], 0, 0))],
            pl.BlockSpec((TM, d), lambda i, s: (i, 0))),
        out_shape=jax.ShapeDtypeStruct((t, d), F32),
        compiler_params=_params(("parallel",), 48),
        name="ab_out_proj",
    )(seq_of_tile, u, z, nb, w_s, bs_tile, w_out, x, mod)


def _gla_in_kernel(seq_ref, x_ref, mod_ref, g_ref, w_ref, wa_ref, zf_ref, zv_ref, a_ref, *, q_scale, n_f32):
    h = _norm_mod(x_ref[...], g_ref[...], mod_ref[3:4, :], mod_ref[4:5, :]).astype(BF16)
    a_ref[...] = jnp.dot(h, wa_ref[...], preferred_element_type=F32)
    for j in range(w_ref.shape[1] // TN):
        z = jnp.dot(h, w_ref[:, j * TN:(j + 1) * TN], preferred_element_type=F32)
        if j == 0:
            z = z * q_scale
        if j < n_f32:
            zf_ref[:, j * TN:(j + 1) * TN] = z
        else:
            zv_ref[:, (j - n_f32) * TN:(j - n_f32 + 1) * TN] = z.astype(BF16)


def _gla_in(x, mod, seq_of_tile, norm_g, w_main, w_a, dk, dv, tm):
    t, d = x.shape
    assert dk == TN and dv % TN == 0
    n_f32 = (2 * dk + dv) // TN
    hk = dk // GLA_HEADS
    return pl.pallas_call(
        functools.partial(_gla_in_kernel, q_scale=float(hk) ** -0.5, n_f32=n_f32),
        grid_spec=_grid_spec(
            1, (t // tm,),
            [pl.BlockSpec((tm, d), lambda i, s: (i, 0)),
             pl.BlockSpec((None, N_MOD, d), lambda i, s: (s[i], 0, 0)),
             pl.BlockSpec((1, d), lambda i, s: (0, 0)),
             _resident(w_main.shape),
             _resident(w_a.shape)],
            [pl.BlockSpec((tm, 2 * dk + dv), lambda i, s: (i, 0)),
             pl.BlockSpec((tm, dv), lambda i, s: (i, 0)),
             pl.BlockSpec((tm, LANES), lambda i, s: (i, 0))]),
        out_shape=[jax.ShapeDtypeStruct((t, 2 * dk + dv), F32), jax.ShapeDtypeStruct((t, dv), BF16),
                   jax.ShapeDtypeStruct((t, LANES), F32)],
        compiler_params=_params(("parallel",), 52),
        name="gla_in_proj",
    )(seq_of_tile, x, mod, norm_g.reshape(1, d), w_main, w_a)


def _log_sigmoid(x):
    return jnp.minimum(x, 0.0) - jnp.log1p(jnp.exp(-jnp.abs(x)))


def _gla_sweep_kernel(order_ref, first_ref, q_ref, k_ref, v_ref, a_ref, wa2_ref, ba_ref, *rest, reverse):
    if reverse:
        prev_ref, o_ref, st_ref = rest
    else:
        o_ref, st_ref = rest
    i = pl.program_id(0)
    L = GLA_CHUNK
    tc, dk = q_ref.shape
    dv = v_ref.shape[1]
    hk, hv = dk // GLA_HEADS, dv // GLA_HEADS

    @pl.when(first_ref[i] == 1)
    def _():
        st_ref[...] = jnp.zeros_like(st_ref)

    gate_pre = jnp.dot(a_ref[...].astype(BF16), wa2_ref[...], preferred_element_type=F32) + ba_ref[...]
    g_all = _log_sigmoid(gate_pre) * np.float32(1.0 / GLA_TAU)

    row = lax.broadcasted_iota(jnp.int32, (L, L), 0)
    col = lax.broadcasted_iota(jnp.int32, (L, L), 1)
    if reverse:
        tri = (col >= row).astype(BF16)
        keep = col > row
        i_ref, i_last = L - 1 - L // 2, 0
    else:
        tri = (col <= row).astype(BF16)
        keep = col <= row
        i_ref, i_last = L // 2, L - 1

    heads = [(slice(h * hk, (h + 1) * hk), slice(h * hv, (h + 1) * hv)) for h in range(GLA_HEADS)]
    nt = (((1,), (1,)), ((), ()))
    tn = (((0,), (0,)), ((), ()))

    nchunks = tc // L
    for c in (range(nchunks - 1, -1, -1) if reverse else range(nchunks)):
        rs = slice(c * L, (c + 1) * L)
        gc = g_all[rs, :]
        g1 = gc.astype(BF16)
        r1 = gc - g1.astype(F32)
        g2 = r1.astype(BF16)
        g3 = (r1 - g2.astype(F32)).astype(BF16)
        b = (jnp.dot(tri, g1, preferred_element_type=F32) + jnp.dot(tri, g2, preferred_element_type=F32)
             + jnp.dot(tri, g3, preferred_element_type=F32))
        b_mid = b[i_ref:i_ref + 1, :]
        b_last = b[i_last:i_last + 1, :]
        qc = q_ref[rs, :]
        kc = k_ref[rs, :]
        qa = (qc * jnp.exp(b - b_mid)).astype(BF16)
        ka = (kc * jnp.exp(b_mid - b)).astype(BF16)
        qx = (qc * jnp.exp(b)).astype(BF16)
        kx = (kc * jnp.exp(b_last - b)).astype(BF16)
        decay = jnp.exp(b_last)
        vc = v_ref[rs, :]
        for h, (ks, vs) in enumerate(heads):
            att = lax.dot_general(qa[:, ks], ka[:, ks], nt, preferred_element_type=F32)
            att = jnp.where(keep, att, 0.0).astype(BF16)
            st = st_ref[h]
            o = (jnp.dot(att, vc[:, vs], preferred_element_type=F32)
                 + lax.dot_general(qx[:, ks], st.astype(BF16), nt, preferred_element_type=F32))
            st_ref[h] = st * decay[:, ks] + lax.dot_general(vc[:, vs], kx[:, ks], tn, preferred_element_type=F32)
            if reverse:
                o = o + prev_ref[rs, vs]
            o_ref[rs, vs] = o


def _gla_sweep(zf, zv, a, wa2, ba, seq_lens, reverse, prev=None):
    t, dv = zv.shape
    dk = wa2.shape[1]
    tc = GLA_BLOCK
    nblk = t // tc
    first = np.zeros(nblk, np.int32)
    base = 0
    for n in seq_lens:
        assert n % tc == 0
        first[(base + n) // tc - 1 if reverse else base // tc] = 1
        base += n
    order = np.arange(nblk, dtype=np.int32)
    if reverse:
        order = order[::-1].copy()
        first = first[order]
    tok = lambda col: (lambda i, order, first: (order[i], col))
    in_specs = [pl.BlockSpec((tc, dk), tok(0)),
                pl.BlockSpec((tc, dk), tok(1)),
                pl.BlockSpec((tc, dv), tok(0)),
                pl.BlockSpec((tc, LANES), tok(0)),
                _resident(wa2.shape),
                _resident(ba.shape)]
    args = [zf, zf, zv, a, wa2, ba]
    if reverse:
        in_specs.append(pl.BlockSpec((tc, dv), tok(0)))
        args.append(prev)
    return pl.pallas_call(
        functools.partial(_gla_sweep_kernel, reverse=reverse),
        grid_spec=_grid_spec(
            2, (nblk,), in_specs, pl.BlockSpec((tc, dv), tok(0)),
            [pltpu.VMEM((GLA_HEADS, dv // GLA_HEADS, dk // GLA_HEADS), F32)]),
        out_shape=jax.ShapeDtypeStruct((t, dv), F32),
        compiler_params=_params(("arbitrary",), 48),
        name="gla_sweep_bwd" if reverse else "gla_sweep_fwd",
    )(jnp.asarray(order), jnp.asarray(first), *args)


def _gla_out_kernel(seq_ref, o_ref, r_ref, ng_ref, wo_ref, x_ref, mod_ref, out_ref):
    dv = o_ref.shape[1]
    hv = dv // GLA_HEADS
    o = o_ref[...]
    parts = []
    for h in range(GLA_HEADS):
        oh = o[:, h * hv:(h + 1) * hv]
        oh = oh * lax.rsqrt(jnp.mean(oh * oh, axis=-1, keepdims=True) + NORM_EPS)
        parts.append(oh * ng_ref[...])
    y = (jnp.concatenate(parts, axis=1) * _silu(r_ref[...])).astype(BF16)
    out_ref[...] = x_ref[...] + mod_ref[5:6, :] * jnp.dot(y, wo_ref[...], preferred_element_type=F32)


def _gla_out(x, mod, seq_of_tile, o, zf, norm_g, w_out, lyr, tm):
    t, d = x.shape
    dv = o.shape[1]
    r_col = (zf.shape[1] - dv) // dv
    assert r_col * dv + dv == zf.shape[1]
    return pl.pallas_call(
        _gla_out_kernel,
        grid_spec=_grid_spec(
            1, (t // tm,),
            [pl.BlockSpec((tm, dv), lambda i, s: (i, 0)),
             pl.BlockSpec((tm, dv), lambda i, s: (i, r_col)),
             pl.BlockSpec((1, dv // GLA_HEADS), lambda i, s: (0, 0)),
             _resident(w_out.shape, (lyr,)),
             pl.BlockSpec((tm, d), lambda i, s: (i, 0)),
             pl.BlockSpec((None, N_MOD, d), lambda i, s: (s[i], 0, 0))],
            pl.BlockSpec((tm, d), lambda i, s: (i, 0))),
        out_shape=jax.ShapeDtypeStruct((t, d), F32),
        compiler_params=_params(("parallel",), 48),
        name="gla_out_proj",
    )(seq_of_tile, o, zf, norm_g.reshape(1, -1), w_out, x, mod)


def _final_norm_kernel(x_ref, g_ref, op_ref, os_ref, *, tiles_p):
    i = pl.program_id(0)
    x = x_ref[...]
    y = x * lax.rsqrt(jnp.mean(x * x, axis=-1, keepdims=True) + NORM_EPS) * g_ref[...]

    @pl.when(i < tiles_p)
    def _():
        op_ref[...] = y

    @pl.when(i >= tiles_p)
    def _():
        os_ref[...] = y


def _final_norm(x, g, n_prompt):
    t, d = x.shape
    assert n_prompt % TM == 0
    tiles_p = n_prompt // TM
    return pl.pallas_call(
        functools.partial(_final_norm_kernel, tiles_p=tiles_p),
        grid=(t // TM,),
        in_specs=[pl.BlockSpec((TM, d), lambda i: (i, 0)), pl.BlockSpec((1, d), lambda i: (0, 0))],
        out_specs=[pl.BlockSpec((TM, d), lambda i: (jnp.minimum(i, tiles_p - 1), 0)),
                   pl.BlockSpec((TM, d), lambda i: (jnp.maximum(i - tiles_p, 0), 0))],
        out_shape=[jax.ShapeDtypeStruct((n_prompt, d), F32), jax.ShapeDtypeStruct((t - n_prompt, d), F32)],
        compiler_params=_params(("arbitrary",), 32),
        name="final_norm",
    )(x, g.reshape(1, d))


def _seq_table(seq_lens, tile):
    out = []
    for s, n in enumerate(seq_lens):
        assert n % tile == 0
        out += [s] * (n // tile)
    return jnp.asarray(np.array(out, np.int32))


def kernel(x_prompt, x_sample, c_prompt, c_sample, mod_w, mod_b, norm_g, ffn_w_gate, ffn_w_up, ffn_w_down,
           ab_w_in, gmlp_ln_g, gmlp_ln_b, gmlp_w_s, gmlp_b_s, na_rpb, ab_w_out,
           gla_w_in, gla_w_a2, gla_b_a, gla_norm_g, gla_w_out, final_norm_g):
    d = x_prompt.shape[-1]
    depth = mod_w.shape[0]
    seq_lens = (x_prompt.shape[1],) * x_prompt.shape[0] + (x_sample.shape[1],) * x_sample.shape[0]
    n_prompt = x_prompt.shape[0] * x_prompt.shape[1]

    xs = [x_prompt.reshape(-1, d), x_sample.reshape(-1, d)]
    c = jnp.concatenate([c_prompt, c_sample], axis=0)
    mod = _modulation(c, mod_w, mod_b)
    seq_tm = _seq_table(seq_lens, TM)
    gla_tm = TM // 2
    seq_gla = _seq_table(seq_lens, gla_tm)

    dk = gla_w_a2.shape[-1]
    dv = gla_w_out.shape[1]

    wg, wu, wd = ffn_w_gate.astype(BF16), ffn_w_up.astype(BF16), ffn_w_down.astype(BF16)
    ab_in_w, ab_out_w, sgu_w = ab_w_in.astype(BF16), ab_w_out.astype(BF16), gmlp_w_s.astype(BF16)
    gla_out_w = gla_w_out.astype(BF16)

    for i in range(depth):
        m = mod[i]
        x = _ffn(xs if i == 0 else [x], m, seq_lens, norm_g[i, 0], wg, wu, wd, (i, 0), 0)
        j = i // 2
        if i % 2 == 0:
            u, z = _ab_in(x, m, seq_tm, norm_g[i, 1], ab_in_w, j, gmlp_ln_g[j], gmlp_ln_b[j])
            nb = _na(z, _na_bias_table(na_rpb[j]), seq_lens)
            x = _ab_out(x, m, seq_tm, u, z, nb, sgu_w, gmlp_b_s[j], ab_out_w, j)
        else:
            w = gla_w_in[j]
            w_main = jnp.concatenate([w[:, :2 * dk], w[:, 2 * dk + dv:2 * dk + 2 * dv], w[:, 2 * dk:2 * dk + dv]],
                                     axis=1).astype(BF16)
            w_a = jnp.pad(w[:, 2 * dk + 2 * dv:], ((0, 0), (0, LANES - 2 * GLA_RANK))).astype(BF16)
            zf, zv, a = _gla_in(x, m, seq_gla, norm_g[i, 1], w_main, w_a, dk, dv, gla_tm)
            outs = None
            for z_dir in range(2):
                wa2 = jnp.zeros((LANES, dk), F32).at[z_dir * GLA_RANK:(z_dir + 1) * GLA_RANK].set(gla_w_a2[j, z_dir])
                outs = _gla_sweep(zf, zv, a, wa2.astype(BF16), gla_b_a[j, z_dir].reshape(1, dk).astype(F32),
                                  seq_lens, reverse=bool(z_dir), prev=outs)
            x = _gla_out(x, m, seq_gla, outs, zf, gla_norm_g[j], gla_out_w, j, gla_tm)
        x = _ffn([x], m, seq_lens, norm_g[i, 2], wg, wu, wd, (i, 1), 6)
    y_prompt, y_sample = _final_norm(x, final_norm_g, n_prompt)
    return (y_prompt.reshape(x_prompt.shape), y_sample.reshape(x_sample.shape))
```

```python
import functools

import numpy as np
import jax
import jax.numpy as jnp
from jax import lax
from jax.experimental import pallas as pl
from jax.experimental.pallas import tpu as pltpu

F32 = jnp.float32
BF16 = jnp.bfloat16

NORM_EPS = 1e-6
N_MOD = 9
GRID_W = 64
GMLP_GROUPS = 8
GMLP_CHUNK = 128
NA_HEADS = 8
NA_KH = 8
NA_KW = 16
GLA_HEADS = 4
GLA_RANK = 16
GLA_TAU = 16.0
GLA_CHUNK = 64
MASK_VALUE = -1e30

LANES = 128
MIB = 2 ** 20

TM = 512
FFN_TILE = (1024, 512)
FFN_SUB = 256
FFN_TILE_SPLIT = (512, 512)
NORM_ROWS = 32
TN = 1024
NA_ROWS = 8
NA_WIN_ROWS = 16
GLA_BLOCK = 512


def _params(dims, vmem_mib):
    return pltpu.CompilerParams(dimension_semantics=dims, vmem_limit_bytes=vmem_mib * MIB)


def _grid_spec(n_prefetch, grid, in_specs, out_specs, scratch=()):
    return pltpu.PrefetchScalarGridSpec(num_scalar_prefetch=n_prefetch, grid=grid, in_specs=in_specs,
                                        out_specs=out_specs, scratch_shapes=list(scratch))


def _resident(shape, lead=()):
    block = (None,) * len(lead) + tuple(shape[len(lead):])
    index = tuple(lead) + (0,) * (len(shape) - len(lead))
    return pl.BlockSpec(block, lambda *_: index, pipeline_mode=pl.Buffered(1))


def _norm_mod(x, g, shift, scale):
    y = x * lax.rsqrt(jnp.mean(x * x, axis=-1, keepdims=True) + NORM_EPS) * g
    return y * (1.0 + scale) + shift


def _silu(x):
    return x * jax.nn.sigmoid(x)


def _gelu(x):
    return 0.5 * x * (1.0 + lax.erf(x * np.float32(np.sqrt(0.5))))


def _mod_kernel(c_ref, w_ref, b_ref, o_ref):
    c = c_ref[...]
    o_ref[...] = jnp.dot(_silu(c).astype(BF16), w_ref[...].astype(BF16),
                         preferred_element_type=F32) + b_ref[...]


def _modulation(c, mod_w, mod_b):
    depth, d, n = mod_w.shape
    nseq = c.shape[0]
    rows = -(-nseq // 16) * 16
    c_pad = jnp.pad(c, ((0, rows - nseq), (0, 0)))
    tn = 1024
    out = pl.pallas_call(
        _mod_kernel,
        grid=(depth, n // tn),
        in_specs=[pl.BlockSpec((rows, d), lambda l, j: (0, 0)),
                  pl.BlockSpec((None, d, tn), lambda l, j: (l, 0, j)),
                  pl.BlockSpec((None, 1, tn), lambda l, j: (l, 0, j))],
        out_specs=pl.BlockSpec((None, rows, tn), lambda l, j: (l, 0, j)),
        out_shape=jax.ShapeDtypeStruct((depth, rows, n), F32),
        compiler_params=_params(("arbitrary", "arbitrary"), 40),
        name="modulation",
    )(c_pad, mod_w, mod_b.reshape(depth, 1, n))
    return out[:, :nseq].reshape(depth, nseq, N_MOD, d)


def _ffn_kernel(seq_ref, *refs, mod_base, tiles_first):
    x_refs, (mod_ref, g_ref, wg_ref, wu_ref, wd_ref, o_ref, h_ref, inv_ref) = refs[:-8], refs[-8:]
    i = pl.program_id(0)
    j = pl.program_id(1)

    def start_tile(x_ref):
        n_chunks = x_ref.shape[0] // NORM_ROWS
        reps = x_ref.shape[1] // LANES

        def chunk(r):
            return pl.ds(pl.multiple_of(r * NORM_ROWS, NORM_ROWS), NORM_ROWS)

        def rms_body(r, carry):
            x = x_ref[chunk(r), :]
            inv = lax.rsqrt(jnp.mean(x * x, axis=-1, keepdims=True) + NORM_EPS)
            inv_ref[chunk(r), :] = jnp.broadcast_to(inv, (NORM_ROWS, LANES))
            return carry

        lax.fori_loop(0, n_chunks, rms_body, 0, unroll=4)
        gain = g_ref[...] * (1.0 + mod_ref[mod_base + 1:mod_base + 2, :])
        shift = mod_ref[mod_base:mod_base + 1, :]

        half = NORM_ROWS // 2

        def mod_body(r, carry):
            rows = pl.ds(pl.multiple_of(r * half, half), half)
            x = x_ref[rows, :]
            inv = jnp.concatenate([inv_ref[rows, :]] * reps, axis=1)
            h_ref[rows, :] = ((x * inv) * gain + shift).astype(BF16)
            o_ref[rows, :] = x
            return carry

        lax.fori_loop(0, 2 * n_chunks, mod_body, 0, unroll=4)

    if len(x_refs) == 1:
        pl.when(j == 0)(lambda: start_tile(x_refs[0]))
    else:
        pl.when(jnp.logical_and(j == 0, i < tiles_first))(lambda: start_tile(x_refs[0]))
        pl.when(jnp.logical_and(j == 0, i >= tiles_first))(lambda: start_tile(x_refs[1]))

    h = h_ref[...]
    res_gate = 0.5 * mod_ref[mod_base + 2:mod_base + 3, :]
    for c in range(wg_ref.shape[1] // FFN_SUB):
        cols = slice(c * FFN_SUB, (c + 1) * FFN_SUB)
        gate = jnp.dot(h, wg_ref[:, cols], preferred_element_type=F32)
        up = jnp.dot(h, wu_ref[:, cols], preferred_element_type=F32)
        a = (_silu(gate) * up).astype(BF16)
        o_ref[...] += res_gate * jnp.dot(a, wd_ref[cols, :], preferred_element_type=F32)


def _ffn_vmem_mib(tm, tf, d, n_x):
    tiles = (n_x + 1) * 2 * tm * d * 4 + tm * d * 2
    weights = 2 * 3 * d * tf * 2
    temps = 3 * tm * FFN_SUB * 4
    return (tiles + weights + temps) // MIB + 7


def _ffn(xs, mod, seq_lens, norm_g, wg, wu, wd, which, mod_base):
    d = xs[0].shape[1]
    t = sum(x.shape[0] for x in xs)
    f = wg.shape[-1]
    lyr, slot = which
    tm, tf = FFN_TILE_SPLIT if len(xs) > 1 else FFN_TILE
    tok = lambda i, j, s: (i, 0)
    if len(xs) == 1:
        tiles_first = None
        x_specs = [pl.BlockSpec((tm, d), tok)]
    else:
        assert xs[0].shape[0] % tm == 0
        tiles_first = xs[0].shape[0] // tm
        x_specs = [pl.BlockSpec((tm, d), lambda i, j, s: (jnp.minimum(i, tiles_first - 1), 0)),
                   pl.BlockSpec((tm, d), lambda i, j, s: (jnp.maximum(i - tiles_first, 0), 0))]
    return pl.pallas_call(
        functools.partial(_ffn_kernel, mod_base=mod_base, tiles_first=tiles_first),
        grid_spec=_grid_spec(
            1, (t // tm, f // tf),
            x_specs + [
             pl.BlockSpec((None, N_MOD, d), lambda i, j, s: (s[i], 0, 0)),
             pl.BlockSpec((1, d), lambda i, j, s: (0, 0)),
             pl.BlockSpec((None, None, d, tf), lambda i, j, s: (lyr, slot, 0, j)),
             pl.BlockSpec((None, None, d, tf), lambda i, j, s: (lyr, slot, 0, j)),
             pl.BlockSpec((None, None, tf, d), lambda i, j, s: (lyr, slot, j, 0))],
            pl.BlockSpec((tm, d), tok),
            [pltpu.VMEM((tm, d), BF16), pltpu.VMEM((tm, LANES), F32)]),
        out_shape=jax.ShapeDtypeStruct((t, d), F32),
        compiler_params=_params(("arbitrary", "arbitrary"), _ffn_vmem_mib(tm, tf, d, len(xs))),
        name="swiglu",
    )(_seq_table(seq_lens, tm), *xs, mod, norm_g.reshape(1, d), wg, wu, wd)


def _ab_in_kernel(seq_ref, x_ref, mod_ref, g_ref, w_ref, lng_ref, lnb_ref, u_ref, z_ref, *, q_scale):
    h = _norm_mod(x_ref[...], g_ref[...], mod_ref[3:4, :], mod_ref[4:5, :]).astype(BF16)

    def col(j):
        return jnp.dot(h, w_ref[:, j * TN:(j + 1) * TN], preferred_element_type=F32)

    u_ref[...] = _gelu(col(0))
    v = _gelu(col(1))
    vc = v - jnp.mean(v, axis=-1, keepdims=True)
    y = vc * lax.rsqrt(jnp.mean(vc * vc, axis=-1, keepdims=True) + NORM_EPS)
    z_ref[:, 0:TN] = (y * lng_ref[...] + lnb_ref[...]).astype(BF16)
    z_ref[:, TN:2 * TN] = (col(2) * q_scale).astype(BF16)
    z_ref[:, 2 * TN:3 * TN] = col(3).astype(BF16)
    z_ref[:, 3 * TN:4 * TN] = col(4).astype(BF16)


def _ab_in(x, mod, seq_of_tile, norm_g, w_in, lyr, ln_g, ln_b):
    t, d = x.shape
    n = w_in.shape[-1]
    d_a = ln_g.shape[0]
    assert d_a == TN and n == 5 * TN
    head_dim = d_a // NA_HEADS
    return pl.pallas_call(
        functools.partial(_ab_in_kernel, q_scale=float(head_dim) ** -0.5),
        grid_spec=_grid_spec(
            1, (t // TM,),
            [pl.BlockSpec((TM, d), lambda i, s: (i, 0)),
             pl.BlockSpec((None, N_MOD, d), lambda i, s: (s[i], 0, 0)),
             pl.BlockSpec((1, d), lambda i, s: (0, 0)),
             _resident(w_in.shape, (lyr,)),
             pl.BlockSpec((1, d_a), lambda i, s: (0, 0)),
             pl.BlockSpec((1, d_a), lambda i, s: (0, 0))],
            [pl.BlockSpec((TM, TN), lambda i, s: (i, 0)),
             pl.BlockSpec((TM, n - TN), lambda i, s: (i, 0))]),
        out_shape=[jax.ShapeDtypeStruct((t, TN), F32), jax.ShapeDtypeStruct((t, n - TN), BF16)],
        compiler_params=_params(("parallel",), 52),
        name="ab_in_proj",
    )(seq_of_tile, x, mod, norm_g.reshape(1, d), w_in, ln_g.reshape(1, d_a), ln_b.reshape(1, d_a))


def _na_kernel(lb_ref, rows_ref, win_ref, q_ref, k_ref, v_ref, bias_ref, o_ref):
    i = pl.program_id(0)
    lb = lb_ref[i]
    rows = rows_ref[i]
    head_dim = q_ref.shape[1] // NA_HEADS
    band = NA_KH * GRID_W
    win_row0 = jnp.clip(lb * NA_ROWS - NA_KH // 2, 0, rows - NA_WIN_ROWS)

    def row_body(rl, carry):
        r = lb * NA_ROWS + rl
        row_start = jnp.clip(r - NA_KH // 2, 0, rows - NA_KH)
        off = pl.multiple_of((row_start - win_row0) * GRID_W, GRID_W)
        shift = row_start - r + (NA_KH - 1)
        q_rows = pl.ds(pl.multiple_of(rl * GRID_W, GRID_W), GRID_W)
        heads = [slice(h * head_dim, (h + 1) * head_dim) for h in range(NA_HEADS)]
        scores = [lax.dot_general(q_ref[q_rows, cols], k_ref[pl.ds(off, band), cols], (((1,), (1,)), ((), ())),
                                  preferred_element_type=F32) for cols in heads]
        probs, norms = [], []
        for h in range(NA_HEADS):
            s = scores[h] + bias_ref[h, pl.ds(shift, 1)][0]
            p = jnp.exp(s - jnp.max(s, axis=-1, keepdims=True))
            norms.append(jnp.sum(p, axis=-1, keepdims=True))
            probs.append(p.astype(BF16))
        for h, cols in enumerate(heads):
            o = jnp.dot(probs[h], v_ref[pl.ds(off, band), cols], preferred_element_type=F32) / norms[h]
            o_ref[q_rows, cols] = o.astype(BF16)
        return carry

    lax.fori_loop(0, NA_ROWS, row_body, 0)


def _na_bias_table(rpb):
    qc = np.arange(GRID_W)
    kc = np.arange(GRID_W)
    cstart = np.clip(qc - NA_KW // 2, 0, GRID_W - NA_KW)
    valid = (kc[None, :] >= cstart[:, None]) & (kc[None, :] < cstart[:, None] + NA_KW)
    col_idx = np.clip(kc[None, :] - qc[:, None] + NA_KW - 1, 0, 2 * NA_KW - 2)
    onehot = (col_idx[None] == np.arange(2 * NA_KW - 1)[:, None, None]).astype(np.float32)
    cols = jnp.einsum('hij,jqk->hiqk', rpb.astype(F32), onehot, precision=lax.Precision.HIGHEST)
    cols = jnp.where(valid[None, None], cols, F32(MASK_VALUE))
    t = jnp.stack([cols[:, s:s + NA_KH] for s in range(NA_KH)], axis=1)
    t = jnp.transpose(t, (0, 1, 3, 2, 4))
    return t.reshape(rpb.shape[0], NA_KH, GRID_W, NA_KH * GRID_W)


def _na(z, bias, seq_lens):
    t = z.shape[0]
    d_b = z.shape[1] // 4
    blk = NA_ROWS * GRID_W
    win = NA_WIN_ROWS * GRID_W
    lb, rows, wstart = [], [], []
    base = 0
    for n in seq_lens:
        assert n % blk == 0 and n >= win
        for b in range(n // blk):
            lb.append(b)
            rows.append(n // GRID_W)
            wstart.append((base + int(np.clip(b * blk - (NA_KH // 2) * GRID_W, 0, n - win))) // GRID_W)
        base += n
    tables = [jnp.asarray(np.array(a, np.int32)) for a in (lb, rows, wstart)]
    return pl.pallas_call(
        _na_kernel,
        grid_spec=_grid_spec(
            3, (t // blk,),
            [pl.BlockSpec((blk, d_b), lambda i, lb, rows, ws: (i, 1)),
             pl.BlockSpec((pl.Element(win), pl.Element(d_b)), lambda i, lb, rows, ws: (ws[i] * GRID_W, 2 * d_b)),
             pl.BlockSpec((pl.Element(win), pl.Element(d_b)), lambda i, lb, rows, ws: (ws[i] * GRID_W, 3 * d_b)),
             _resident(bias.shape)],
            pl.BlockSpec((blk, d_b), lambda i, lb, rows, ws: (i, 0))),
        out_shape=jax.ShapeDtypeStruct((t, d_b), BF16),
        compiler_params=_params(("parallel",), 32),
        name="neighborhood_attention",
    )(*tables, z, z, z, bias)


def _ab_out_kernel(seq_ref, u_ref, z_ref, nb_ref, ws_ref, bs_ref, wo_ref, x_ref, mod_ref, o_ref):
    tm, d_a = u_ref.shape
    nch = tm // GMLP_CHUNK
    cg = d_a // GMLP_GROUPS
    vs = z_ref[...]
    s_blocks = [[None] * GMLP_GROUPS for _ in range(nch)]
    for g in range(GMLP_GROUPS):
        cat = jnp.concatenate([vs[n * GMLP_CHUNK:(n + 1) * GMLP_CHUNK, g * cg:(g + 1) * cg] for n in range(nch)],
                              axis=1)
        sg = jnp.dot(ws_ref[g], cat, preferred_element_type=F32)
        for n in range(nch):
            s_blocks[n][g] = sg[:, n * cg:(n + 1) * cg] + bs_ref[g]
    s = jnp.concatenate([jnp.concatenate(row, axis=1) for row in s_blocks], axis=0)
    a = (u_ref[...] * s).astype(BF16)
    y = (jnp.dot(a, wo_ref[0:d_a, :], preferred_element_type=F32)
         + jnp.dot(nb_ref[...], wo_ref[d_a:, :], preferred_element_type=F32))
    o_ref[...] = x_ref[...] + mod_ref[5:6, :] * y


def _ab_out(x, mod, seq_of_tile, u, z, nb, w_s, b_s, w_out, lyr):
    t, d = x.shape
    d_a = u.shape[1]
    cg = d_a // GMLP_GROUPS
    assert cg == LANES and GMLP_CHUNK == LANES
    bs_tile = jnp.broadcast_to(b_s.astype(F32)[:, :, None], (GMLP_GROUPS, GMLP_CHUNK, cg))
    return pl.pallas_call(
        _ab_out_kernel,
        grid_spec=_grid_spec(
            1, (t // TM,),
            [pl.BlockSpec((TM, d_a), lambda i, s: (i, 0)),
             pl.BlockSpec((TM, d_a), lambda i, s: (i, 0)),
             pl.BlockSpec((TM, nb.shape[1]), lambda i, s: (i, 0)),
             _resident(w_s.shape, (lyr,)),
             _resident(bs_tile.shape),
             _resident(w_out.shape, (lyr,)),
             pl.BlockSpec((TM, d), lambda i, s: (i, 0)),
             pl.BlockSpec((None, N_MOD, d), lambda i, s: (s[i], 0, 0))],
            pl.BlockSpec((TM, d), lambda i, s: (i, 0))),
        out_shape=jax.ShapeDtypeStruct((t, d), F32),
        compiler_params=_params(("parallel",), 48),
        name="ab_out_proj",
    )(seq_of_tile, u, z, nb, w_s, bs_tile, w_out, x, mod)


def _gla_in_kernel(seq_ref, x_ref, mod_ref, g_ref, w_ref, wa_ref, zf_ref, zv_ref, a_ref, *, q_scale, n_f32):
    h = _norm_mod(x_ref[...], g_ref[...], mod_ref[3:4, :], mod_ref[4:5, :]).astype(BF16)
    a_ref[...] = jnp.dot(h, wa_ref[...], preferred_element_type=F32)
    for j in range(w_ref.shape[1] // TN):
        z = jnp.dot(h, w_ref[:, j * TN:(j + 1) * TN], preferred_element_type=F32)
        if j == 0:
            z = z * q_scale
        if j < n_f32:
            zf_ref[:, j * TN:(j + 1) * TN] = z
        else:
            zv_ref[:, (j - n_f32) * TN:(j - n_f32 + 1) * TN] = z.astype(BF16)


def _gla_in(x, mod, seq_of_tile, norm_g, w_main, w_a, dk, dv, tm):
    t, d = x.shape
    assert dk == TN and dv % TN == 0
    n_f32 = (2 * dk + dv) // TN
    hk = dk // GLA_HEADS
    return pl.pallas_call(
        functools.partial(_gla_in_kernel, q_scale=float(hk) ** -0.5, n_f32=n_f32),
        grid_spec=_grid_spec(
            1, (t // tm,),
            [pl.BlockSpec((tm, d), lambda i, s: (i, 0)),
             pl.BlockSpec((None, N_MOD, d), lambda i, s: (s[i], 0, 0)),
             pl.BlockSpec((1, d), lambda i, s: (0, 0)),
             _resident(w_main.shape),
             _resident(w_a.shape)],
            [pl.BlockSpec((tm, 2 * dk + dv), lambda i, s: (i, 0)),
             pl.BlockSpec((tm, dv), lambda i, s: (i, 0)),
             pl.BlockSpec((tm, LANES), lambda i, s: (i, 0))]),
        out_shape=[jax.ShapeDtypeStruct((t, 2 * dk + dv), F32), jax.ShapeDtypeStruct((t, dv), BF16),
                   jax.ShapeDtypeStruct((t, LANES), F32)],
        compiler_params=_params(("parallel",), 52),
        name="gla_in_proj",
    )(seq_of_tile, x, mod, norm_g.reshape(1, d), w_main, w_a)


def _log_sigmoid(x):
    return jnp.minimum(x, 0.0) - jnp.log1p(jnp.exp(-jnp.abs(x)))


def _gla_sweep_kernel(order_ref, first_ref, q_ref, k_ref, v_ref, a_ref, wa2_ref, ba_ref, *rest, reverse):
    if reverse:
        prev_ref, o_ref, st_ref = rest
    else:
        o_ref, st_ref = rest
    i = pl.program_id(0)
    L = GLA_CHUNK
    tc, dk = q_ref.shape
    dv = v_ref.shape[1]
    hk, hv = dk // GLA_HEADS, dv // GLA_HEADS

    @pl.when(first_ref[i] == 1)
    def _():
        st_ref[...] = jnp.zeros_like(st_ref)

    gate_pre = jnp.dot(a_ref[...].astype(BF16), wa2_ref[...], preferred_element_type=F32) + ba_ref[...]
    g_all = _log_sigmoid(gate_pre) * np.float32(1.0 / GLA_TAU)

    P = 2 * L
    lo, hi = slice(0, L), slice(L, P)
    row = lax.broadcasted_iota(jnp.int32, (L, L), 0)
    col = lax.broadcasted_iota(jnp.int32, (L, L), 1)
    prow = lax.broadcasted_iota(jnp.int32, (P, P), 0)
    pcol = lax.broadcasted_iota(jnp.int32, (P, P), 1)
    same_chunk = (prow >= L) == (pcol >= L)
    if reverse:
        tri = jnp.logical_and(same_chunk, pcol >= prow).astype(BF16)
        keep = col > row
        i_ref, i_last = L - 1 - L // 2, 0
    else:
        tri = jnp.logical_and(same_chunk, pcol <= prow).astype(BF16)
        keep = col <= row
        i_ref, i_last = L // 2, L - 1

    heads = [(slice(h * hk, (h + 1) * hk), slice(h * hv, (h + 1) * hv)) for h in range(GLA_HEADS)]
    nt = (((1,), (1,)), ((), ()))
    tn = (((0,), (0,)), ((), ()))

    def per_chunk(v_lo, v_hi):
        return jnp.concatenate([jnp.broadcast_to(v_lo, (L, dk)), jnp.broadcast_to(v_hi, (L, dk))], axis=0)

    npairs = tc // P
    ones = jnp.ones((1, dk), F32)
    zero_block = jnp.zeros((L, L), F32)
    for p in (range(npairs - 1, -1, -1) if reverse else range(npairs)):
        rs = slice(p * P, (p + 1) * P)
        gp = g_all[rs, :]
        g1 = gp.astype(BF16)
        r1 = gp - g1.astype(F32)
        g2 = r1.astype(BF16)
        g3 = (r1 - g2.astype(F32)).astype(BF16)
        b = (jnp.dot(tri, g1, preferred_element_type=F32) + jnp.dot(tri, g2, preferred_element_type=F32)
             + jnp.dot(tri, g3, preferred_element_type=F32))
        b_mid = per_chunk(b[i_ref:i_ref + 1, :], b[L + i_ref:L + i_ref + 1, :])
        bl_lo, bl_hi = b[i_last:i_last + 1, :], b[L + i_last:L + i_last + 1, :]
        d_lo, d_hi = jnp.exp(bl_lo), jnp.exp(bl_hi)
        if reverse:
            q_mul, k_mul = per_chunk(d_hi, ones), per_chunk(ones, d_lo)
        else:
            q_mul, k_mul = per_chunk(ones, d_lo), per_chunk(d_hi, ones)
        qp = q_ref[rs, :]
        kp = k_ref[rs, :]
        qa = (qp * jnp.exp(b - b_mid)).astype(BF16)
        ka = (kp * jnp.exp(b_mid - b)).astype(BF16)
        qx = qp * jnp.exp(b)
        kx = kp * jnp.exp(per_chunk(bl_lo, bl_hi) - b)
        qx_b, kx_b = qx.astype(BF16), kx.astype(BF16)
        q_state, k_state = (qx * q_mul).astype(BF16), (kx * k_mul).astype(BF16)
        decay = d_lo * d_hi
        vp = v_ref[rs, :]
        for h, (ks, vs) in enumerate(heads):
            att_lo = jnp.where(keep, lax.dot_general(qa[lo, ks], ka[lo, ks], nt, preferred_element_type=F32), 0.0)
            att_hi = jnp.where(keep, lax.dot_general(qa[hi, ks], ka[hi, ks], nt, preferred_element_type=F32), 0.0)
            if reverse:
                cross = lax.dot_general(qx_b[lo, ks], kx_b[hi, ks], nt, preferred_element_type=F32)
                scores = [[att_lo, cross], [zero_block, att_hi]]
            else:
                cross = lax.dot_general(qx_b[hi, ks], kx_b[lo, ks], nt, preferred_element_type=F32)
                scores = [[att_lo, zero_block], [cross, att_hi]]
            scores = jnp.concatenate([jnp.concatenate(r, axis=1) for r in scores], axis=0).astype(BF16)
            st = st_ref[h]
            o = (jnp.dot(scores, vp[:, vs], preferred_element_type=F32)
                 + lax.dot_general(q_state[:, ks], st.astype(BF16), nt, preferred_element_type=F32))
            st_ref[h] = st * decay[:, ks] + lax.dot_general(vp[:, vs], k_state[:, ks], tn,
                                                            preferred_element_type=F32)
            if reverse:
                o = o + prev_ref[rs, vs]
            o_ref[rs, vs] = o


def _gla_sweep(zf, zv, a, wa2, ba, seq_lens, reverse, prev=None):
    t, dv = zv.shape
    dk = wa2.shape[1]
    tc = GLA_BLOCK
    nblk = t // tc
    first = np.zeros(nblk, np.int32)
    base = 0
    for n in seq_lens:
        assert n % tc == 0
        first[(base + n) // tc - 1 if reverse else base // tc] = 1
        base += n
    order = np.arange(nblk, dtype=np.int32)
    if reverse:
        order = order[::-1].copy()
        first = first[order]
    tok = lambda col: (lambda i, order, first: (order[i], col))
    in_specs = [pl.BlockSpec((tc, dk), tok(0)),
                pl.BlockSpec((tc, dk), tok(1)),
                pl.BlockSpec((tc, dv), tok(0)),
                pl.BlockSpec((tc, LANES), tok(0)),
                _resident(wa2.shape),
                _resident(ba.shape)]
    args = [zf, zf, zv, a, wa2, ba]
    if reverse:
        in_specs.append(pl.BlockSpec((tc, dv), tok(0)))
        args.append(prev)
    return pl.pallas_call(
        functools.partial(_gla_sweep_kernel, reverse=reverse),
        grid_spec=_grid_spec(
            2, (nblk,), in_specs, pl.BlockSpec((tc, dv), tok(0)),
            [pltpu.VMEM((GLA_HEADS, dv // GLA_HEADS, dk // GLA_HEADS), F32)]),
        out_shape=jax.ShapeDtypeStruct((t, dv), F32),
        compiler_params=_params(("arbitrary",), 48),
        name="gla_sweep_bwd" if reverse else "gla_sweep_fwd",
    )(jnp.asarray(order), jnp.asarray(first), *args)


def _gla_out_kernel(seq_ref, o_ref, r_ref, ng_ref, wo_ref, x_ref, mod_ref, out_ref):
    dv = o_ref.shape[1]
    hv = dv // GLA_HEADS
    o = o_ref[...]
    parts = []
    for h in range(GLA_HEADS):
        oh = o[:, h * hv:(h + 1) * hv]
        oh = oh * lax.rsqrt(jnp.mean(oh * oh, axis=-1, keepdims=True) + NORM_EPS)
        parts.append(oh * ng_ref[...])
    y = (jnp.concatenate(parts, axis=1) * _silu(r_ref[...])).astype(BF16)
    out_ref[...] = x_ref[...] + mod_ref[5:6, :] * jnp.dot(y, wo_ref[...], preferred_element_type=F32)


def _gla_out(x, mod, seq_of_tile, o, zf, norm_g, w_out, lyr, tm):
    t, d = x.shape
    dv = o.shape[1]
    r_col = (zf.shape[1] - dv) // dv
    assert r_col * dv + dv == zf.shape[1]
    return pl.pallas_call(
        _gla_out_kernel,
        grid_spec=_grid_spec(
            1, (t // tm,),
            [pl.BlockSpec((tm, dv), lambda i, s: (i, 0)),
             pl.BlockSpec((tm, dv), lambda i, s: (i, r_col)),
             pl.BlockSpec((1, dv // GLA_HEADS), lambda i, s: (0, 0)),
             _resident(w_out.shape, (lyr,)),
             pl.BlockSpec((tm, d), lambda i, s: (i, 0)),
             pl.BlockSpec((None, N_MOD, d), lambda i, s: (s[i], 0, 0))],
            pl.BlockSpec((tm, d), lambda i, s: (i, 0))),
        out_shape=jax.ShapeDtypeStruct((t, d), F32),
        compiler_params=_params(("parallel",), 52),
        name="gla_out_proj",
    )(seq_of_tile, o, zf, norm_g.reshape(1, -1), w_out, x, mod)


def _final_norm_kernel(x_ref, g_ref, op_ref, os_ref, *, tiles_p):
    i = pl.program_id(0)
    x = x_ref[...]
    y = x * lax.rsqrt(jnp.mean(x * x, axis=-1, keepdims=True) + NORM_EPS) * g_ref[...]

    @pl.when(i < tiles_p)
    def _():
        op_ref[...] = y

    @pl.when(i >= tiles_p)
    def _():
        os_ref[...] = y


def _final_norm(x, g, n_prompt):
    t, d = x.shape
    assert n_prompt % TM == 0
    tiles_p = n_prompt // TM
    return pl.pallas_call(
        functools.partial(_final_norm_kernel, tiles_p=tiles_p),
        grid=(t // TM,),
        in_specs=[pl.BlockSpec((TM, d), lambda i: (i, 0)), pl.BlockSpec((1, d), lambda i: (0, 0))],
        out_specs=[pl.BlockSpec((TM, d), lambda i: (jnp.minimum(i, tiles_p - 1), 0)),
                   pl.BlockSpec((TM, d), lambda i: (jnp.maximum(i - tiles_p, 0), 0))],
        out_shape=[jax.ShapeDtypeStruct((n_prompt, d), F32), jax.ShapeDtypeStruct((t - n_prompt, d), F32)],
        compiler_params=_params(("arbitrary",), 32),
        name="final_norm",
    )(x, g.reshape(1, d))


def _seq_table(seq_lens, tile):
    out = []
    for s, n in enumerate(seq_lens):
        assert n % tile == 0
        out += [s] * (n // tile)
    return jnp.asarray(np.array(out, np.int32))


def kernel(x_prompt, x_sample, c_prompt, c_sample, mod_w, mod_b, norm_g, ffn_w_gate, ffn_w_up, ffn_w_down,
           ab_w_in, gmlp_ln_g, gmlp_ln_b, gmlp_w_s, gmlp_b_s, na_rpb, ab_w_out,
           gla_w_in, gla_w_a2, gla_b_a, gla_norm_g, gla_w_out, final_norm_g):
    d = x_prompt.shape[-1]
    depth = mod_w.shape[0]
    seq_lens = (x_prompt.shape[1],) * x_prompt.shape[0] + (x_sample.shape[1],) * x_sample.shape[0]
    n_prompt = x_prompt.shape[0] * x_prompt.shape[1]

    xs = [x_prompt.reshape(-1, d), x_sample.reshape(-1, d)]
    c = jnp.concatenate([c_prompt, c_sample], axis=0)
    mod = _modulation(c, mod_w, mod_b)
    seq_tm = _seq_table(seq_lens, TM)
    gla_tm = TM // 2
    seq_gla = _seq_table(seq_lens, gla_tm)

    dk = gla_w_a2.shape[-1]
    dv = gla_w_out.shape[1]

    wg, wu, wd = ffn_w_gate.astype(BF16), ffn_w_up.astype(BF16), ffn_w_down.astype(BF16)
    ab_in_w, ab_out_w, sgu_w = ab_w_in.astype(BF16), ab_w_out.astype(BF16), gmlp_w_s.astype(BF16)
    gla_out_w = gla_w_out.astype(BF16)

    for i in range(depth):
        m = mod[i]
        x = _ffn(xs if i == 0 else [x], m, seq_lens, norm_g[i, 0], wg, wu, wd, (i, 0), 0)
        j = i // 2
        if i % 2 == 0:
            u, z = _ab_in(x, m, seq_tm, norm_g[i, 1], ab_in_w, j, gmlp_ln_g[j], gmlp_ln_b[j])
            nb = _na(z, _na_bias_table(na_rpb[j]), seq_lens)
            x = _ab_out(x, m, seq_tm, u, z, nb, sgu_w, gmlp_b_s[j], ab_out_w, j)
        else:
            w = gla_w_in[j]
            w_main = jnp.concatenate([w[:, :2 * dk], w[:, 2 * dk + dv:2 * dk + 2 * dv], w[:, 2 * dk:2 * dk + dv]],
                                     axis=1).astype(BF16)
            w_a = jnp.pad(w[:, 2 * dk + 2 * dv:], ((0, 0), (0, LANES - 2 * GLA_RANK))).astype(BF16)
            zf, zv, a = _gla_in(x, m, seq_gla, norm_g[i, 1], w_main, w_a, dk, dv, gla_tm)
            outs = None
            for z_dir in range(2):
                wa2 = jnp.zeros((LANES, dk), F32).at[z_dir * GLA_RANK:(z_dir + 1) * GLA_RANK].set(gla_w_a2[j, z_dir])
                outs = _gla_sweep(zf, zv, a, wa2.astype(BF16), gla_b_a[j, z_dir].reshape(1, dk).astype(F32),
                                  seq_lens, reverse=bool(z_dir), prev=outs)
            x = _gla_out(x, m, seq_tm, outs, zf, gla_norm_g[j], gla_out_w, j, TM)
        x = _ffn([x], m, seq_lens, norm_g[i, 2], wg, wu, wd, (i, 1), 6)
    y_prompt, y_sample = _final_norm(x, final_norm_g, n_prompt)
    return (y_prompt.reshape(x_prompt.shape), y_sample.reshape(x_sample.shape))
```

```python
import functools

import numpy as np
import jax
import jax.numpy as jnp
from jax import lax
from jax.experimental import pallas as pl
from jax.experimental.pallas import tpu as pltpu

F32 = jnp.float32
BF16 = jnp.bfloat16

NORM_EPS = 1e-6
N_MOD = 9
GRID_W = 64
GMLP_GROUPS = 8
GMLP_CHUNK = 128
NA_HEADS = 8
NA_KH = 8
NA_KW = 16
GLA_HEADS = 4
GLA_RANK = 16
GLA_TAU = 16.0
GLA_CHUNK = 64
MASK_VALUE = -1e30

LANES = 128
MIB = 2 ** 20

TM = 512
FFN_TILE = (1024, 512)
FFN_SUB = 256
FFN_TILE_SPLIT = (512, 512)
NORM_ROWS = 32
TN = 1024
NA_ROWS = 8
NA_WIN_ROWS = 16
GLA_BLOCK = 512


def _params(dims, vmem_mib):
    return pltpu.CompilerParams(dimension_semantics=dims, vmem_limit_bytes=vmem_mib * MIB)


def _grid_spec(n_prefetch, grid, in_specs, out_specs, scratch=()):
    return pltpu.PrefetchScalarGridSpec(num_scalar_prefetch=n_prefetch, grid=grid, in_specs=in_specs,
                                        out_specs=out_specs, scratch_shapes=list(scratch))


def _resident(shape, lead=()):
    block = (None,) * len(lead) + tuple(shape[len(lead):])
    index = tuple(lead) + (0,) * (len(shape) - len(lead))
    return pl.BlockSpec(block, lambda *_: index, pipeline_mode=pl.Buffered(1))


def _norm_mod(x, g, shift, scale):
    y = x * lax.rsqrt(jnp.mean(x * x, axis=-1, keepdims=True) + NORM_EPS) * g
    return y * (1.0 + scale) + shift


def _silu(x):
    return x * jax.nn.sigmoid(x)


def _gelu(x):
    return 0.5 * x * (1.0 + lax.erf(x * np.float32(np.sqrt(0.5))))


def _mod_kernel(c_ref, w_ref, b_ref, o_ref):
    c = c_ref[...]
    o_ref[...] = jnp.dot(_silu(c).astype(BF16), w_ref[...].astype(BF16),
                         preferred_element_type=F32) + b_ref[...]


def _modulation(c, mod_w, mod_b):
    depth, d, n = mod_w.shape
    nseq = c.shape[0]
    rows = -(-nseq // 16) * 16
    c_pad = jnp.pad(c, ((0, rows - nseq), (0, 0)))
    tn = 1024
    out = pl.pallas_call(
        _mod_kernel,
        grid=(depth, n // tn),
        in_specs=[pl.BlockSpec((rows, d), lambda l, j: (0, 0)),
                  pl.BlockSpec((None, d, tn), lambda l, j: (l, 0, j)),
                  pl.BlockSpec((None, 1, tn), lambda l, j: (l, 0, j))],
        out_specs=pl.BlockSpec((None, rows, tn), lambda l, j: (l, 0, j)),
        out_shape=jax.ShapeDtypeStruct((depth, rows, n), F32),
        compiler_params=_params(("arbitrary", "arbitrary"), 40),
        name="modulation",
    )(c_pad, mod_w, mod_b.reshape(depth, 1, n))
    return out[:, :nseq].reshape(depth, nseq, N_MOD, d)


def _ffn_kernel(seq_ref, *refs, mod_base, tiles_first):
    x_refs, (mod_ref, g_ref, wg_ref, wu_ref, wd_ref, o_ref, h_ref, inv_ref) = refs[:-8], refs[-8:]
    i = pl.program_id(0)
    j = pl.program_id(1)

    def start_tile(x_ref):
        n_chunks = x_ref.shape[0] // NORM_ROWS
        reps = x_ref.shape[1] // LANES

        def chunk(r):
            return pl.ds(pl.multiple_of(r * NORM_ROWS, NORM_ROWS), NORM_ROWS)

        def rms_body(r, carry):
            x = x_ref[chunk(r), :]
            inv = lax.rsqrt(jnp.mean(x * x, axis=-1, keepdims=True) + NORM_EPS)
            inv_ref[chunk(r), :] = jnp.broadcast_to(inv, (NORM_ROWS, LANES))
            return carry

        lax.fori_loop(0, n_chunks, rms_body, 0, unroll=4)
        gain = g_ref[...] * (1.0 + mod_ref[mod_base + 1:mod_base + 2, :])
        shift = mod_ref[mod_base:mod_base + 1, :]

        half = NORM_ROWS // 2

        def mod_body(r, carry):
            rows = pl.ds(pl.multiple_of(r * half, half), half)
            x = x_ref[rows, :]
            inv = jnp.concatenate([inv_ref[rows, :]] * reps, axis=1)
            h_ref[rows, :] = ((x * inv) * gain + shift).astype(BF16)
            o_ref[rows, :] = x
            return carry

        lax.fori_loop(0, 2 * n_chunks, mod_body, 0, unroll=4)

    if len(x_refs) == 1:
        pl.when(j == 0)(lambda: start_tile(x_refs[0]))
    else:
        pl.when(jnp.logical_and(j == 0, i < tiles_first))(lambda: start_tile(x_refs[0]))
        pl.when(jnp.logical_and(j == 0, i >= tiles_first))(lambda: start_tile(x_refs[1]))

    h = h_ref[...]
    res_gate = 0.5 * mod_ref[mod_base + 2:mod_base + 3, :]
    for c in range(wg_ref.shape[1] // FFN_SUB):
        cols = slice(c * FFN_SUB, (c + 1) * FFN_SUB)
        gate = jnp.dot(h, wg_ref[:, cols], preferred_element_type=F32)
        up = jnp.dot(h, wu_ref[:, cols], preferred_element_type=F32)
        a = (_silu(gate) * up).astype(BF16)
        o_ref[...] += res_gate * jnp.dot(a, wd_ref[cols, :], preferred_element_type=F32)


def _ffn_vmem_mib(tm, tf, d, n_x):
    tiles = (n_x + 1) * 2 * tm * d * 4 + tm * d * 2
    weights = 2 * 3 * d * tf * 2
    temps = 3 * tm * FFN_SUB * 4
    return (tiles + weights + temps) // MIB + 7


def _ffn(xs, mod, seq_lens, norm_g, wg, wu, wd, which, mod_base):
    d = xs[0].shape[1]
    t = sum(x.shape[0] for x in xs)
    f = wg.shape[-1]
    lyr, slot = which
    tm, tf = FFN_TILE_SPLIT if len(xs) > 1 else FFN_TILE
    tok = lambda i, j, s: (i, 0)
    if len(xs) == 1:
        tiles_first = None
        x_specs = [pl.BlockSpec((tm, d), tok)]
    else:
        assert xs[0].shape[0] % tm == 0
        tiles_first = xs[0].shape[0] // tm
        x_specs = [pl.BlockSpec((tm, d), lambda i, j, s: (jnp.minimum(i, tiles_first - 1), 0)),
                   pl.BlockSpec((tm, d), lambda i, j, s: (jnp.maximum(i - tiles_first, 0), 0))]
    return pl.pallas_call(
        functools.partial(_ffn_kernel, mod_base=mod_base, tiles_first=tiles_first),
        grid_spec=_grid_spec(
            1, (t // tm, f // tf),
            x_specs + [
             pl.BlockSpec((None, N_MOD, d), lambda i, j, s: (s[i], 0, 0)),
             pl.BlockSpec((1, d), lambda i, j, s: (0, 0)),
             pl.BlockSpec((None, None, d, tf), lambda i, j, s: (lyr, slot, 0, j)),
             pl.BlockSpec((None, None, d, tf), lambda i, j, s: (lyr, slot, 0, j)),
             pl.BlockSpec((None, None, tf, d), lambda i, j, s: (lyr, slot, j, 0))],
            pl.BlockSpec((tm, d), tok),
            [pltpu.VMEM((tm, d), BF16), pltpu.VMEM((tm, LANES), F32)]),
        out_shape=jax.ShapeDtypeStruct((t, d), F32),
        compiler_params=_params(("arbitrary", "arbitrary"), _ffn_vmem_mib(tm, tf, d, len(xs))),
        name="swiglu",
    )(_seq_table(seq_lens, tm), *xs, mod, norm_g.reshape(1, d), wg, wu, wd)


def _ab_in_kernel(seq_ref, x_ref, mod_ref, g_ref, w_ref, lng_ref, lnb_ref, u_ref, z_ref, *, q_scale):
    h = _norm_mod(x_ref[...], g_ref[...], mod_ref[3:4, :], mod_ref[4:5, :]).astype(BF16)

    def col(j):
        return jnp.dot(h, w_ref[:, j * TN:(j + 1) * TN], preferred_element_type=F32)

    u_ref[...] = _gelu(col(0))
    v = _gelu(col(1))
    vc = v - jnp.mean(v, axis=-1, keepdims=True)
    y = vc * lax.rsqrt(jnp.mean(vc * vc, axis=-1, keepdims=True) + NORM_EPS)
    z_ref[:, 0:TN] = (y * lng_ref[...] + lnb_ref[...]).astype(BF16)
    z_ref[:, TN:2 * TN] = (col(2) * q_scale).astype(BF16)
    z_ref[:, 2 * TN:3 * TN] = col(3).astype(BF16)
    z_ref[:, 3 * TN:4 * TN] = col(4).astype(BF16)


def _ab_in(x, mod, seq_of_tile, norm_g, w_in, lyr, ln_g, ln_b):
    t, d = x.shape
    n = w_in.shape[-1]
    d_a = ln_g.shape[0]
    assert d_a == TN and n == 5 * TN
    head_dim = d_a // NA_HEADS
    return pl.pallas_call(
        functools.partial(_ab_in_kernel, q_scale=float(head_dim) ** -0.5),
        grid_spec=_grid_spec(
            1, (t // TM,),
            [pl.BlockSpec((TM, d), lambda i, s: (i, 0)),
             pl.BlockSpec((None, N_MOD, d), lambda i, s: (s[i], 0, 0)),
             pl.BlockSpec((1, d), lambda i, s: (0, 0)),
             _resident(w_in.shape, (lyr,)),
             pl.BlockSpec((1, d_a), lambda i, s: (0, 0)),
             pl.BlockSpec((1, d_a), lambda i, s: (0, 0))],
            [pl.BlockSpec((TM, TN), lambda i, s: (i, 0)),
             pl.BlockSpec((TM, n - TN), lambda i, s: (i, 0))]),
        out_shape=[jax.ShapeDtypeStruct((t, TN), F32), jax.ShapeDtypeStruct((t, n - TN), BF16)],
        compiler_params=_params(("parallel",), 52),
        name="ab_in_proj",
    )(seq_of_tile, x, mod, norm_g.reshape(1, d), w_in, ln_g.reshape(1, d_a), ln_b.reshape(1, d_a))


def _na_kernel(lb_ref, rows_ref, win_ref, q_ref, k_ref, v_ref, bias_ref, o_ref):
    i = pl.program_id(0)
    lb = lb_ref[i]
    rows = rows_ref[i]
    head_dim = q_ref.shape[1] // NA_HEADS
    band = NA_KH * GRID_W
    win_row0 = jnp.clip(lb * NA_ROWS - NA_KH // 2, 0, rows - NA_WIN_ROWS)

    def row_body(rl, carry):
        r = lb * NA_ROWS + rl
        row_start = jnp.clip(r - NA_KH // 2, 0, rows - NA_KH)
        off = pl.multiple_of((row_start - win_row0) * GRID_W, GRID_W)
        shift = row_start - r + (NA_KH - 1)
        q_rows = pl.ds(pl.multiple_of(rl * GRID_W, GRID_W), GRID_W)
        heads = [slice(h * head_dim, (h + 1) * head_dim) for h in range(NA_HEADS)]
        scores = [lax.dot_general(q_ref[q_rows, cols], k_ref[pl.ds(off, band), cols], (((1,), (1,)), ((), ())),
                                  preferred_element_type=F32) for cols in heads]
        probs, norms = [], []
        for h in range(NA_HEADS):
            s = scores[h] + bias_ref[h, pl.ds(shift, 1)][0]
            p = jnp.exp(s - jnp.max(s, axis=-1, keepdims=True))
            norms.append(jnp.sum(p, axis=-1, keepdims=True))
            probs.append(p.astype(BF16))
        for h, cols in enumerate(heads):
            o = jnp.dot(probs[h], v_ref[pl.ds(off, band), cols], preferred_element_type=F32) / norms[h]
            o_ref[q_rows, cols] = o.astype(BF16)
        return carry

    lax.fori_loop(0, NA_ROWS, row_body, 0)


def _na_bias_table(rpb):
    qc = np.arange(GRID_W)
    kc = np.arange(GRID_W)
    cstart = np.clip(qc - NA_KW // 2, 0, GRID_W - NA_KW)
    valid = (kc[None, :] >= cstart[:, None]) & (kc[None, :] < cstart[:, None] + NA_KW)
    col_idx = np.clip(kc[None, :] - qc[:, None] + NA_KW - 1, 0, 2 * NA_KW - 2)
    onehot = (col_idx[None] == np.arange(2 * NA_KW - 1)[:, None, None]).astype(np.float32)
    cols = jnp.einsum('hij,jqk->hiqk', rpb.astype(F32), onehot, precision=lax.Precision.HIGHEST)
    cols = jnp.where(valid[None, None], cols, F32(MASK_VALUE))
    t = jnp.stack([cols[:, s:s + NA_KH] for s in range(NA_KH)], axis=1)
    t = jnp.transpose(t, (0, 1, 3, 2, 4))
    return t.reshape(rpb.shape[0], NA_KH, GRID_W, NA_KH * GRID_W)


def _na(z, bias, seq_lens):
    t = z.shape[0]
    d_b = z.shape[1] // 4
    blk = NA_ROWS * GRID_W
    win = NA_WIN_ROWS * GRID_W
    lb, rows, wstart = [], [], []
    base = 0
    for n in seq_lens:
        assert n % blk == 0 and n >= win
        for b in range(n // blk):
            lb.append(b)
            rows.append(n // GRID_W)
            wstart.append((base + int(np.clip(b * blk - (NA_KH // 2) * GRID_W, 0, n - win))) // GRID_W)
        base += n
    tables = [jnp.asarray(np.array(a, np.int32)) for a in (lb, rows, wstart)]
    return pl.pallas_call(
        _na_kernel,
        grid_spec=_grid_spec(
            3, (t // blk,),
            [pl.BlockSpec((blk, d_b), lambda i, lb, rows, ws: (i, 1)),
             pl.BlockSpec((pl.Element(win), pl.Element(d_b)), lambda i, lb, rows, ws: (ws[i] * GRID_W, 2 * d_b)),
             pl.BlockSpec((pl.Element(win), pl.Element(d_b)), lambda i, lb, rows, ws: (ws[i] * GRID_W, 3 * d_b)),
             _resident(bias.shape)],
            pl.BlockSpec((blk, d_b), lambda i, lb, rows, ws: (i, 0))),
        out_shape=jax.ShapeDtypeStruct((t, d_b), BF16),
        compiler_params=_params(("parallel",), 32),
        name="neighborhood_attention",
    )(*tables, z, z, z, bias)


def _ab_out_kernel(seq_ref, u_ref, z_ref, nb_ref, ws_ref, bs_ref, wo_ref, x_ref, mod_ref, o_ref):
    tm, d_a = u_ref.shape
    nch = tm // GMLP_CHUNK
    cg = d_a // GMLP_GROUPS
    vs = z_ref[...]
    s_blocks = [[None] * GMLP_GROUPS for _ in range(nch)]
    for g in range(GMLP_GROUPS):
        cat = jnp.concatenate([vs[n * GMLP_CHUNK:(n + 1) * GMLP_CHUNK, g * cg:(g + 1) * cg] for n in range(nch)],
                              axis=1)
        sg = jnp.dot(ws_ref[g], cat, preferred_element_type=F32)
        for n in range(nch):
            s_blocks[n][g] = sg[:, n * cg:(n + 1) * cg] + bs_ref[g]
    s = jnp.concatenate([jnp.concatenate(row, axis=1) for row in s_blocks], axis=0)
    a = (u_ref[...] * s).astype(BF16)
    y = (jnp.dot(a, wo_ref[0:d_a, :], preferred_element_type=F32)
         + jnp.dot(nb_ref[...], wo_ref[d_a:, :], preferred_element_type=F32))
    o_ref[...] = x_ref[...] + mod_ref[5:6, :] * y


def _ab_out(x, mod, seq_of_tile, u, z, nb, w_s, b_s, w_out, lyr):
    t, d = x.shape
    d_a = u.shape[1]
    cg = d_a // GMLP_GROUPS
    assert cg == LANES and GMLP_CHUNK == LANES
    bs_tile = jnp.broadcast_to(b_s.astype(F32)[:, :, None], (GMLP_GROUPS, GMLP_CHUNK, cg))
    return pl.pallas_call(
        _ab_out_kernel,
        grid_spec=_grid_spec(
            1, (t // TM,),
            [pl.BlockSpec((TM, d_a), lambda i, s: (i, 0)),
             pl.BlockSpec((TM, d_a), lambda i, s: (i, 0)),
             pl.BlockSpec((TM, nb.shape[1]), lambda i, s: (i, 0)),
             _resident(w_s.shape, (lyr,)),
             _resident(bs_tile.shape),
             _resident(w_out.shape, (lyr,)),
             pl.BlockSpec((TM, d), lambda i, s: (i, 0)),
             pl.BlockSpec((None, N_MOD, d), lambda i, s: (s[i], 0, 0))],
            pl.BlockSpec((TM, d), lambda i, s: (i, 0))),
        out_shape=jax.ShapeDtypeStruct((t, d), F32),
        compiler_params=_params(("parallel",), 48),
        name="ab_out_proj",
    )(seq_of_tile, u, z, nb, w_s, bs_tile, w_out, x, mod)


def _gla_in_kernel(seq_ref, x_ref, mod_ref, g_ref, w_ref, wa_ref, zf_ref, zv_ref, a_ref, *, q_scale, n_f32):
    h = _norm_mod(x_ref[...], g_ref[...], mod_ref[3:4, :], mod_ref[4:5, :]).astype(BF16)
    a_ref[...] = jnp.dot(h, wa_ref[...], preferred_element_type=F32)
    for j in range(w_ref.shape[1] // TN):
        z = jnp.dot(h, w_ref[:, j * TN:(j + 1) * TN], preferred_element_type=F32)
        if j == 0:
            z = z * q_scale
        if j < n_f32:
            zf_ref[:, j * TN:(j + 1) * TN] = z
        else:
            zv_ref[:, (j - n_f32) * TN:(j - n_f32 + 1) * TN] = z.astype(BF16)


def _gla_in(x, mod, seq_of_tile, norm_g, w_main, w_a, dk, dv, tm):
    t, d = x.shape
    assert dk == TN and dv % TN == 0
    n_f32 = (2 * dk + dv) // TN
    hk = dk // GLA_HEADS
    return pl.pallas_call(
        functools.partial(_gla_in_kernel, q_scale=float(hk) ** -0.5, n_f32=n_f32),
        grid_spec=_grid_spec(
            1, (t // tm,),
            [pl.BlockSpec((tm, d), lambda i, s: (i, 0)),
             pl.BlockSpec((None, N_MOD, d), lambda i, s: (s[i], 0, 0)),
             pl.BlockSpec((1, d), lambda i, s: (0, 0)),
             _resident(w_main.shape),
             _resident(w_a.shape)],
            [pl.BlockSpec((tm, 2 * dk + dv), lambda i, s: (i, 0)),
             pl.BlockSpec((tm, dv), lambda i, s: (i, 0)),
             pl.BlockSpec((tm, LANES), lambda i, s: (i, 0))]),
        out_shape=[jax.ShapeDtypeStruct((t, 2 * dk + dv), F32), jax.ShapeDtypeStruct((t, dv), BF16),
                   jax.ShapeDtypeStruct((t, LANES), F32)],
        compiler_params=_params(("parallel",), 52),
        name="gla_in_proj",
    )(seq_of_tile, x, mod, norm_g.reshape(1, d), w_main, w_a)


def _log_sigmoid(x):
    return jnp.minimum(x, 0.0) - jnp.log(1.0 + jnp.exp(-jnp.abs(x)))


def _gla_sweep_kernel(order_ref, first_ref, q_ref, k_ref, v_ref, a_ref, wa2_ref, ba_ref, *rest, reverse):
    if reverse:
        prev_ref, o_ref, st_ref = rest
    else:
        o_ref, st_ref = rest
    i = pl.program_id(0)
    L = GLA_CHUNK
    tc, dk = q_ref.shape
    dv = v_ref.shape[1]
    hk, hv = dk // GLA_HEADS, dv // GLA_HEADS

    @pl.when(first_ref[i] == 1)
    def _():
        st_ref[...] = jnp.zeros_like(st_ref)

    gate_pre = jnp.dot(a_ref[...].astype(BF16), wa2_ref[...], preferred_element_type=F32) + ba_ref[...]
    g_all = _log_sigmoid(gate_pre) * np.float32(1.0 / GLA_TAU)

    P = 2 * L
    lo, hi = slice(0, L), slice(L, P)
    row = lax.broadcasted_iota(jnp.int32, (L, L), 0)
    col = lax.broadcasted_iota(jnp.int32, (L, L), 1)
    prow = lax.broadcasted_iota(jnp.int32, (P, P), 0)
    pcol = lax.broadcasted_iota(jnp.int32, (P, P), 1)
    same_chunk = (prow >= L) == (pcol >= L)
    if reverse:
        tri = jnp.logical_and(same_chunk, pcol >= prow).astype(BF16)
        keep = col > row
        i_ref, i_last = L - 1 - L // 2, 0
    else:
        tri = jnp.logical_and(same_chunk, pcol <= prow).astype(BF16)
        keep = col <= row
        i_ref, i_last = L // 2, L - 1

    heads = [(slice(h * hk, (h + 1) * hk), slice(h * hv, (h + 1) * hv)) for h in range(GLA_HEADS)]
    nt = (((1,), (1,)), ((), ()))
    tn = (((0,), (0,)), ((), ()))

    def per_chunk(v_lo, v_hi):
        return jnp.concatenate([jnp.broadcast_to(v_lo, (L, dk)), jnp.broadcast_to(v_hi, (L, dk))], axis=0)

    npairs = tc // P
    ones = jnp.ones((1, dk), F32)
    zero_block = jnp.zeros((L, L), F32)
    for p in (range(npairs - 1, -1, -1) if reverse else range(npairs)):
        rs = slice(p * P, (p + 1) * P)
        gp = g_all[rs, :]
        g1 = gp.astype(BF16)
        r1 = gp - g1.astype(F32)
        g2 = r1.astype(BF16)
        g3 = (r1 - g2.astype(F32)).astype(BF16)
        b = (jnp.dot(tri, g1, preferred_element_type=F32) + jnp.dot(tri, g2, preferred_element_type=F32)
             + jnp.dot(tri, g3, preferred_element_type=F32))
        b_mid = per_chunk(b[i_ref:i_ref + 1, :], b[L + i_ref:L + i_ref + 1, :])
        bl_lo, bl_hi = b[i_last:i_last + 1, :], b[L + i_last:L + i_last + 1, :]
        d_lo, d_hi = jnp.exp(bl_lo), jnp.exp(bl_hi)
        if reverse:
            q_mul, k_mul = per_chunk(d_hi, ones), per_chunk(ones, d_lo)
        else:
            q_mul, k_mul = per_chunk(ones, d_lo), per_chunk(d_hi, ones)
        qp = q_ref[rs, :]
        kp = k_ref[rs, :]
        qa = (qp * jnp.exp(b - b_mid)).astype(BF16)
        ka = (kp * jnp.exp(b_mid - b)).astype(BF16)
        qx = qp * jnp.exp(b)
        kx = kp * jnp.exp(per_chunk(bl_lo, bl_hi) - b)
        qx_b, kx_b = qx.astype(BF16), kx.astype(BF16)
        q_state, k_state = (qx * q_mul).astype(BF16), (kx * k_mul).astype(BF16)
        decay = d_lo * d_hi
        vp = v_ref[rs, :]
        for h, (ks, vs) in enumerate(heads):
            att_lo = jnp.where(keep, lax.dot_general(qa[lo, ks], ka[lo, ks], nt, preferred_element_type=F32), 0.0)
            att_hi = jnp.where(keep, lax.dot_general(qa[hi, ks], ka[hi, ks], nt, preferred_element_type=F32), 0.0)
            if reverse:
                cross = lax.dot_general(qx_b[lo, ks], kx_b[hi, ks], nt, preferred_element_type=F32)
                scores = [[att_lo, cross], [zero_block, att_hi]]
            else:
                cross = lax.dot_general(qx_b[hi, ks], kx_b[lo, ks], nt, preferred_element_type=F32)
                scores = [[att_lo, zero_block], [cross, att_hi]]
            scores = jnp.concatenate([jnp.concatenate(r, axis=1) for r in scores], axis=0).astype(BF16)
            st = st_ref[h]
            o = (jnp.dot(scores, vp[:, vs], preferred_element_type=F32)
                 + lax.dot_general(q_state[:, ks], st.astype(BF16), nt, preferred_element_type=F32))
            st_ref[h] = st * decay[:, ks] + lax.dot_general(vp[:, vs], k_state[:, ks], tn,
                                                            preferred_element_type=F32)
            if reverse:
                o = o + prev_ref[rs, vs]
            o_ref[rs, vs] = o


def _gla_sweep(zf, zv, a, wa2, ba, seq_lens, reverse, prev=None):
    t, dv = zv.shape
    dk = wa2.shape[1]
    tc = GLA_BLOCK
    nblk = t // tc
    first = np.zeros(nblk, np.int32)
    base = 0
    for n in seq_lens:
        assert n % tc == 0
        first[(base + n) // tc - 1 if reverse else base // tc] = 1
        base += n
    order = np.arange(nblk, dtype=np.int32)
    if reverse:
        order = order[::-1].copy()
        first = first[order]
    tok = lambda col: (lambda i, order, first: (order[i], col))
    in_specs = [pl.BlockSpec((tc, dk), tok(0)),
                pl.BlockSpec((tc, dk), tok(1)),
                pl.BlockSpec((tc, dv), tok(0)),
                pl.BlockSpec((tc, LANES), tok(0)),
                _resident(wa2.shape),
                _resident(ba.shape)]
    args = [zf, zf, zv, a, wa2, ba]
    if reverse:
        in_specs.append(pl.BlockSpec((tc, dv), tok(0)))
        args.append(prev)
    return pl.pallas_call(
        functools.partial(_gla_sweep_kernel, reverse=reverse),
        grid_spec=_grid_spec(
            2, (nblk,), in_specs, pl.BlockSpec((tc, dv), tok(0)),
            [pltpu.VMEM((GLA_HEADS, dv // GLA_HEADS, dk // GLA_HEADS), F32)]),
        out_shape=jax.ShapeDtypeStruct((t, dv), F32),
        compiler_params=_params(("arbitrary",), 48),
        name="gla_sweep_bwd" if reverse else "gla_sweep_fwd",
    )(jnp.asarray(order), jnp.asarray(first), *args)


def _gla_out_kernel(seq_ref, o_ref, r_ref, ng_ref, wo_ref, x_ref, mod_ref, out_ref):
    dv = o_ref.shape[1]
    hv = dv // GLA_HEADS
    o = o_ref[...]
    parts = []
    for h in range(GLA_HEADS):
        oh = o[:, h * hv:(h + 1) * hv]
        oh = oh * lax.rsqrt(jnp.mean(oh * oh, axis=-1, keepdims=True) + NORM_EPS)
        parts.append(oh * ng_ref[...])
    y = (jnp.concatenate(parts, axis=1) * _silu(r_ref[...])).astype(BF16)
    out_ref[...] = x_ref[...] + mod_ref[5:6, :] * jnp.dot(y, wo_ref[...], preferred_element_type=F32)


def _gla_out(x, mod, seq_of_tile, o, zf, norm_g, w_out, lyr, tm):
    t, d = x.shape
    dv = o.shape[1]
    r_col = (zf.shape[1] - dv) // dv
    assert r_col * dv + dv == zf.shape[1]
    return pl.pallas_call(
        _gla_out_kernel,
        grid_spec=_grid_spec(
            1, (t // tm,),
            [pl.BlockSpec((tm, dv), lambda i, s: (i, 0)),
             pl.BlockSpec((tm, dv), lambda i, s: (i, r_col)),
             pl.BlockSpec((1, dv // GLA_HEADS), lambda i, s: (0, 0)),
             _resident(w_out.shape, (lyr,)),
             pl.BlockSpec((tm, d), lambda i, s: (i, 0)),
             pl.BlockSpec((None, N_MOD, d), lambda i, s: (s[i], 0, 0))],
            pl.BlockSpec((tm, d), lambda i, s: (i, 0))),
        out_shape=jax.ShapeDtypeStruct((t, d), F32),
        compiler_params=_params(("parallel",), 52),
        name="gla_out_proj",
    )(seq_of_tile, o, zf, norm_g.reshape(1, -1), w_out, x, mod)


def _final_norm_kernel(x_ref, g_ref, op_ref, os_ref, *, tiles_p):
    i = pl.program_id(0)
    x = x_ref[...]
    y = x * lax.rsqrt(jnp.mean(x * x, axis=-1, keepdims=True) + NORM_EPS) * g_ref[...]

    @pl.when(i < tiles_p)
    def _():
        op_ref[...] = y

    @pl.when(i >= tiles_p)
    def _():
        os_ref[...] = y


def _final_norm(x, g, n_prompt):
    t, d = x.shape
    assert n_prompt % TM == 0
    tiles_p = n_prompt // TM
    return pl.pallas_call(
        functools.partial(_final_norm_kernel, tiles_p=tiles_p),
        grid=(t // TM,),
        in_specs=[pl.BlockSpec((TM, d), lambda i: (i, 0)), pl.BlockSpec((1, d), lambda i: (0, 0))],
        out_specs=[pl.BlockSpec((TM, d), lambda i: (jnp.minimum(i, tiles_p - 1), 0)),
                   pl.BlockSpec((TM, d), lambda i: (jnp.maximum(i - tiles_p, 0), 0))],
        out_shape=[jax.ShapeDtypeStruct((n_prompt, d), F32), jax.ShapeDtypeStruct((t - n_prompt, d), F32)],
        compiler_params=_params(("arbitrary",), 32),
        name="final_norm",
    )(x, g.reshape(1, d))


def _seq_table(seq_lens, tile):
    out = []
    for s, n in enumerate(seq_lens):
        assert n % tile == 0
        out += [s] * (n // tile)
    return jnp.asarray(np.array(out, np.int32))


def kernel(x_prompt, x_sample, c_prompt, c_sample, mod_w, mod_b, norm_g, ffn_w_gate, ffn_w_up, ffn_w_down,
           ab_w_in, gmlp_ln_g, gmlp_ln_b, gmlp_w_s, gmlp_b_s, na_rpb, ab_w_out,
           gla_w_in, gla_w_a2, gla_b_a, gla_norm_g, gla_w_out, final_norm_g):
    d = x_prompt.shape[-1]
    depth = mod_w.shape[0]
    seq_lens = (x_prompt.shape[1],) * x_prompt.shape[0] + (x_sample.shape[1],) * x_sample.shape[0]
    n_prompt = x_prompt.shape[0] * x_prompt.shape[1]

    xs = [x_prompt.reshape(-1, d), x_sample.reshape(-1, d)]
    c = jnp.concatenate([c_prompt, c_sample], axis=0)
    mod = _modulation(c, mod_w, mod_b)
    seq_tm = _seq_table(seq_lens, TM)
    gla_tm = TM // 2
    seq_gla = _seq_table(seq_lens, gla_tm)

    dk = gla_w_a2.shape[-1]
    dv = gla_w_out.shape[1]

    wg, wu, wd = ffn_w_gate.astype(BF16), ffn_w_up.astype(BF16), ffn_w_down.astype(BF16)
    ab_in_w, ab_out_w, sgu_w = ab_w_in.astype(BF16), ab_w_out.astype(BF16), gmlp_w_s.astype(BF16)
    gla_out_w = gla_w_out.astype(BF16)

    for i in range(depth):
        m = mod[i]
        x = _ffn(xs if i == 0 else [x], m, seq_lens, norm_g[i, 0], wg, wu, wd, (i, 0), 0)
        j = i // 2
        if i % 2 == 0:
            u, z = _ab_in(x, m, seq_tm, norm_g[i, 1], ab_in_w, j, gmlp_ln_g[j], gmlp_ln_b[j])
            nb = _na(z, _na_bias_table(na_rpb[j]), seq_lens)
            x = _ab_out(x, m, seq_tm, u, z, nb, sgu_w, gmlp_b_s[j], ab_out_w, j)
        else:
            w = gla_w_in[j]
            w_main = jnp.concatenate([w[:, :2 * dk], w[:, 2 * dk + dv:2 * dk + 2 * dv], w[:, 2 * dk:2 * dk + dv]],
                                     axis=1).astype(BF16)
            w_a = jnp.pad(w[:, 2 * dk + 2 * dv:], ((0, 0), (0, LANES - 2 * GLA_RANK))).astype(BF16)
            zf, zv, a = _gla_in(x, m, seq_gla, norm_g[i, 1], w_main, w_a, dk, dv, gla_tm)
            outs = None
            for z_dir in range(2):
                wa2 = jnp.zeros((LANES, dk), F32).at[z_dir * GLA_RANK:(z_dir + 1) * GLA_RANK].set(gla_w_a2[j, z_dir])
                outs = _gla_sweep(zf, zv, a, wa2.astype(BF16), gla_b_a[j, z_dir].reshape(1, dk).astype(F32),
                                  seq_lens, reverse=bool(z_dir), prev=outs)
            x = _gla_out(x, m, seq_tm, outs, zf, gla_norm_g[j], gla_out_w, j, TM)
        x = _ffn([x], m, seq_lens, norm_g[i, 2], wg, wu, wd, (i, 1), 6)
    y_prompt, y_sample = _final_norm(x, final_norm_g, n_prompt)
    return (y_prompt.reshape(x_prompt.shape), y_sample.reshape(x_sample.shape))
```
